```python
import math
import jax, jax.numpy as jnp
from jax import lax
import numpy as np

D_MODEL = 1024
BATCH = 8
SEQ = 4096
DEPTH = 2

CONV_WIDTH = 512
CONV_A_K = 3
N_HEADS = 8
HEAD_DIM = 64
ATTN_WIDTH = N_HEADS * HEAD_DIM
ROT_DIM = HEAD_DIM // 4
ROPE_THETA = 500000.0
MOBA_BLOCK = 256
MOBA_TOPK = 3
Q_CHUNK = 64
LRU_WIDTH = 512
LRU_BLOCKS = 8
LRU_BLOCK_DIM = LRU_WIDTH // LRU_BLOCKS
CONV_C_K = 4
LRU_C = 8.0
N_BRANCHES = 3
IN_COLS = 3 * CONV_WIDTH + 3 * ATTN_WIDTH + 2 * LRU_WIDTH + N_BRANCHES * D_MODEL
D_FF = 2816
N_EXPERTS = 8
TOP_K = 2
EXPERT_FF = 3584
EPS = 1e-6
NEG = -1e30

kernel_name = "hybrid_gated_conv_moba_rglru_moe"


def rms_norm(x, g):
    xf = x.astype(jnp.float32)
    y = xf * lax.rsqrt(jnp.mean(xf * xf, axis=-1, keepdims=True) + EPS)
    return (y * g.astype(jnp.float32)).astype(x.dtype)


def causal_depthwise_conv(u, w):
    k_w, c = w.shape
    return lax.conv_general_dilated(
        u, w[:, None, :].astype(u.dtype), window_strides=(1,), padding=[(k_w - 1, 0)],
        dimension_numbers=("NWC", "WIO", "NWC"), feature_group_count=c)


def partial_rotary(t, pos):
    half = ROT_DIM // 2
    inv_freq = ROPE_THETA ** (-jnp.arange(0, ROT_DIM, 2, dtype=jnp.float32) / ROT_DIM)
    ang = pos.astype(jnp.float32)[:, None] * inv_freq[None, :]
    cos = jnp.cos(ang)[None, :, None, :]
    sin = jnp.sin(ang)[None, :, None, :]
    r1 = t[..., :half].astype(jnp.float32)
    r2 = t[..., half:ROT_DIM].astype(jnp.float32)
    rot = jnp.concatenate([r1 * cos - r2 * sin, r2 * cos + r1 * sin], axis=-1).astype(t.dtype)
    return jnp.concatenate([rot, t[..., ROT_DIM:]], axis=-1)


def moba_attention(q, k, v):
    b, s, h, dh = q.shape
    s_pad = -(-s // MOBA_BLOCK) * MOBA_BLOCK
    pad = ((0, 0), (0, s_pad - s), (0, 0), (0, 0))
    q, k, v = [jnp.pad(t, pad).transpose(0, 2, 1, 3) for t in (q, k, v)]
    nb = s_pad // MOBA_BLOCK
    n_sel = min(MOBA_TOPK, nb)
    kb = k.reshape(b, h, nb, MOBA_BLOCK, dh)
    vb = v.reshape(b, h, nb, MOBA_BLOCK, dh)
    k_mean = jnp.mean(kb.astype(jnp.float32), axis=3)
    n_chunks = s_pad // Q_CHUNK
    chunks_per_block = MOBA_BLOCK // Q_CHUNK
    q_chunks = q.reshape(b, h, n_chunks, Q_CHUNK, dh).transpose(2, 0, 1, 3, 4)
    b_idx = jnp.arange(b)[:, None, None, None]
    h_idx = jnp.arange(h)[None, :, None, None]
    scale = dh ** -0.5

    def chunk(args):
        q_c, c = args
        blk = c // chunks_per_block
        q_pos = c * Q_CHUNK + jnp.arange(Q_CHUNK)
        gate = jnp.einsum("bhqd,bhnd->bhqn", q_c.astype(jnp.float32), k_mean)
        gate = jnp.where(jnp.arange(nb) < blk, gate, NEG)
        _, sel = lax.top_k(gate, n_sel)
        k_sel = kb[b_idx, h_idx, sel]
        v_sel = vb[b_idx, h_idx, sel]
        s_sel = jnp.einsum("bhqd,bhqkjd->bhqkj", q_c, k_sel).astype(jnp.float32) * scale
        s_sel = jnp.where((jnp.arange(n_sel) < blk)[:, None], s_sel, NEG)
        k_own = lax.dynamic_index_in_dim(kb, blk, axis=2, keepdims=False)
        v_own = lax.dynamic_index_in_dim(vb, blk, axis=2, keepdims=False)
        k_pos = blk * MOBA_BLOCK + jnp.arange(MOBA_BLOCK)
        s_own = jnp.einsum("bhqd,bhjd->bhqj", q_c, k_own).astype(jnp.float32) * scale
        s_own = jnp.where(k_pos[None, :] <= q_pos[:, None], s_own, NEG)
        scores = jnp.concatenate([s_sel.reshape(b, h, Q_CHUNK, n_sel * MOBA_BLOCK), s_own], axis=-1)
        p = jax.nn.softmax(scores, axis=-1).astype(v.dtype)
        p_sel = p[..., :n_sel * MOBA_BLOCK].reshape(b, h, Q_CHUNK, n_sel, MOBA_BLOCK)
        p_own = p[..., n_sel * MOBA_BLOCK:]
        return (jnp.einsum("bhqkj,bhqkjd->bhqd", p_sel, v_sel)
                + jnp.einsum("bhqj,bhjd->bhqd", p_own, v_own))

    out = lax.map(chunk, (q_chunks, jnp.arange(n_chunks)))
    out = out.transpose(1, 0, 3, 2, 4).reshape(b, s_pad, h * dh)
    return out[:, :s]


def rg_lru(xc, w_r, b_r, w_i, b_i, lam):
    bsz, s, _ = xc.shape
    xb = xc.reshape(bsz, s, LRU_BLOCKS, LRU_BLOCK_DIM)
    r = jax.nn.sigmoid((jnp.einsum("bshi,hij->bshj", xb, w_r).reshape(bsz, s, LRU_WIDTH) + b_r).astype(jnp.float32))
    i = jax.nn.sigmoid((jnp.einsum("bshi,hij->bshj", xb, w_i).reshape(bsz, s, LRU_WIDTH) + b_i).astype(jnp.float32))
    log_a = -LRU_C * r * jax.nn.softplus(-lam.astype(jnp.float32))
    a = jnp.exp(log_a)
    u = jnp.sqrt(-jnp.expm1(2.0 * log_a)) * (i * xc.astype(jnp.float32))

    def combine(c1, c2):
        a1, b1 = c1
        a2, b2 = c2
        return a1 * a2, a2 * b1 + b2

    _, hs = lax.associative_scan(combine, (a, u), axis=1)
    return hs.astype(xc.dtype)


def swiglu(h, w_gate, w_up, w_down):
    return (jax.nn.silu(h @ w_gate) * (h @ w_up)) @ w_down


def moe_swiglu(h, router, w_gate, w_up, w_down):
    bsz, s, d = h.shape
    t = h.reshape(-1, d)
    logits = (t @ router).astype(jnp.float32)
    top_v, top_i = lax.top_k(logits, TOP_K)
    w = jax.nn.softmax(top_v, axis=-1)
    comb = jnp.sum(jax.nn.one_hot(top_i, N_EXPERTS, dtype=jnp.float32) * w[..., None], axis=1)

    def expert_step(acc, params):
        wg, wu, wd, c_e = params
        return acc + c_e[:, None].astype(t.dtype) * swiglu(t, wg, wu, wd), None

    out, _ = lax.scan(expert_step, jnp.zeros_like(t), (w_gate, w_up, w_down, comb.T))
    return out.reshape(bsz, s, d)


def setup_inputs(seed: int = 0) -> dict:
    key = jax.random.key(seed)
    ks = iter(jax.random.split(key, 32))
    L = DEPTH
    n_dense = (DEPTH + 1) // 2
    n_moe = DEPTH // 2

    def nrm(shape, fan_in):
        return jax.random.normal(next(ks), shape, jnp.float32) * (fan_in ** -0.5)

    def gain(shape):
        return 1.0 + 0.02 * jax.random.normal(next(ks), shape, jnp.float32)

    def bias(shape):
        return 0.02 * jax.random.normal(next(ks), shape, jnp.float32)

    a0 = jax.random.uniform(next(ks), (L, LRU_WIDTH), jnp.float32, 0.9, 0.999)
    return {
        "x": jax.random.normal(next(ks), (BATCH, SEQ, D_MODEL), jnp.float32),
        "norm_mix": gain((L, D_MODEL)),
        "w_in": nrm((L, D_MODEL, IN_COLS), D_MODEL),
        "conv_a_w": nrm((L, CONV_A_K, CONV_WIDTH), CONV_A_K),
        "wo_a": nrm((L, CONV_WIDTH, D_MODEL), CONV_WIDTH),
        "q_norm": gain((L, HEAD_DIM)),
        "k_norm": gain((L, HEAD_DIM)),
        "wo_b": nrm((L, ATTN_WIDTH, D_MODEL), ATTN_WIDTH),
        "conv_c_w": nrm((L, CONV_C_K, LRU_WIDTH), CONV_C_K),
        "conv_c_b": bias((L, LRU_WIDTH)),
        "rg_w_r": nrm((L, LRU_BLOCKS, LRU_BLOCK_DIM, LRU_BLOCK_DIM), LRU_BLOCK_DIM),
        "rg_b_r": bias((L, LRU_WIDTH)),
        "rg_w_i": nrm((L, LRU_BLOCKS, LRU_BLOCK_DIM, LRU_BLOCK_DIM), LRU_BLOCK_DIM),
        "rg_b_i": bias((L, LRU_WIDTH)),
        "rg_lambda": jnp.log(a0 / (1.0 - a0)),
        "wo_c": nrm((L, LRU_WIDTH, D_MODEL), LRU_WIDTH),
        "w_o": nrm((L, D_MODEL, D_MODEL), D_MODEL),
        "norm_ffn": gain((L, D_MODEL)),
        "ffn_w_gate": nrm((n_dense, D_MODEL, D_FF), D_MODEL),
        "ffn_w_up": nrm((n_dense, D_MODEL, D_FF), D_MODEL),
        "ffn_w_down": nrm((n_dense, D_FF, D_MODEL), D_FF),
        "moe_router": nrm((n_moe, D_MODEL, N_EXPERTS), D_MODEL),
        "moe_w_gate": nrm((n_moe, N_EXPERTS, D_MODEL, EXPERT_FF), D_MODEL),
        "moe_w_up": nrm((n_moe, N_EXPERTS, D_MODEL, EXPERT_FF), D_MODEL),
        "moe_w_down": nrm((n_moe, N_EXPERTS, EXPERT_FF, D_MODEL), EXPERT_FF),
    }


def reference(x, norm_mix, w_in, conv_a_w, wo_a, q_norm, k_norm, wo_b, conv_c_w, conv_c_b,
              rg_w_r, rg_b_r, rg_w_i, rg_b_i, rg_lambda, wo_c, w_o, norm_ffn,
              ffn_w_gate, ffn_w_up, ffn_w_down, moe_router, moe_w_gate, moe_w_up, moe_w_down):
    bsz, s, d = x.shape
    pos = jnp.arange(s, dtype=jnp.int32)
    sizes = [CONV_WIDTH] * 3 + [ATTN_WIDTH] * 3 + [LRU_WIDTH] * 2 + [N_BRANCHES * D_MODEL]
    cuts = list(np.cumsum(sizes)[:-1])
    for l in range(DEPTH):
        h = rms_norm(x, norm_mix[l])
        p = h @ w_in[l]
        xa, ba, ca, q, k, v, xc, gc, gates = jnp.split(p, cuts, axis=-1)
        y_a = (ba * causal_depthwise_conv(ca * xa, conv_a_w[l])) @ wo_a[l]
        q = partial_rotary(rms_norm(q.reshape(bsz, s, N_HEADS, HEAD_DIM), q_norm[l]), pos)
        k = partial_rotary(rms_norm(k.reshape(bsz, s, N_HEADS, HEAD_DIM), k_norm[l]), pos)
        v = v.reshape(bsz, s, N_HEADS, HEAD_DIM)
        y_b = moba_attention(q, k, v) @ wo_b[l]
        xc = causal_depthwise_conv(xc, conv_c_w[l]) + conv_c_b[l]
        hc = rg_lru(xc, rg_w_r[l], rg_b_r[l], rg_w_i[l], rg_b_i[l], rg_lambda[l])
        y_c = (jax.nn.gelu(gc, approximate=True) * hc) @ wo_c[l]
        g = jax.nn.sigmoid(gates.reshape(bsz, s, N_BRANCHES, d))
        merged = g[:, :, 0] * y_a + g[:, :, 1] * y_b + g[:, :, 2] * y_c
        x = x + merged @ w_o[l]
        h = rms_norm(x, norm_ffn[l])
        if l % 2 == 0:
            x = x + swiglu(h, ffn_w_gate[l // 2], ffn_w_up[l // 2], ffn_w_down[l // 2])
        else:
            x = x + moe_swiglu(h, moe_router[l // 2], moe_w_gate[l // 2], moe_w_up[l // 2], moe_w_down[l // 2])
    return x
```

```python
import functools
import math

import jax
import jax.numpy as jnp
from jax import lax
from jax.experimental import pallas as pl
from jax.experimental.pallas import tpu as pltpu

F32 = jnp.float32
BF16 = jnp.bfloat16

N_HEADS = 8
HEAD_DIM = 64
ROT_DIM = HEAD_DIM // 4
ROPE_THETA = 500000.0
MOBA_BLOCK = 256
MOBA_TOPK = 3
LRU_C = 8.0
N_BRANCHES = 3
TOP_K = 2
EPS = 1e-6
NEG = -1e30

LANES = 128
HEADS_PER_TILE = LANES // HEAD_DIM
VMEM_LIMIT = 56 * 1024 * 1024


def _cparams(sem):
    return pltpu.CompilerParams(dimension_semantics=sem, vmem_limit_bytes=VMEM_LIMIT)


def _split_bf16(x):
    hi = x.astype(BF16)
    lo = (x - hi.astype(F32)).astype(BF16)
    return hi, lo


def _rms(x, g):
    ms = jnp.mean(x * x, axis=-1, keepdims=True)
    return x * lax.rsqrt(ms + EPS) * g


def _inproj_kernel(x_ref, g_ref, w_ref, o_ref, h_scr):
    @pl.when(pl.program_id(1) == 0)
    def _():
        h_scr[...] = _rms(x_ref[...], g_ref[...]).astype(BF16)

    o_ref[...] = jnp.dot(h_scr[...], w_ref[...], preferred_element_type=F32).astype(o_ref.dtype)


def _inproj(x2, g, w, tm=1024, tn=1024):
    t, d = x2.shape
    n = w.shape[1]
    return pl.pallas_call(
        _inproj_kernel,
        grid=(t // tm, n // tn),
        in_specs=[pl.BlockSpec((tm, d), lambda i, j: (i, 0)),
                  pl.BlockSpec((1, d), lambda i, j: (0, 0)),
                  pl.BlockSpec((d, tn), lambda i, j: (0, j))],
        out_specs=pl.BlockSpec((tm, tn), lambda i, j: (i, j)),
        out_shape=jax.ShapeDtypeStruct((t, n), BF16),
        scratch_shapes=[pltpu.VMEM((tm, d), BF16)],
        compiler_params=_cparams(("parallel", "arbitrary")),
        name="inproj",
    )(x2, g, w)


HALO = 8


def _seqmix_kernel(xa_ref, ba_ref, ca_ref, xc_ref, gc_ref, wa_ref, wc_ref, bc_ref,
                   wr_ref, br_ref, wi_ref, bi_ref, lam_ref, ya_ref, yc_ref,
                   ua_buf, xc_buf, h_carry):
    ts = xa_ref.shape[0]
    ka = wa_ref.shape[0]
    kc = wc_ref.shape[0]

    @pl.when(pl.program_id(1) == 0)
    def _():
        ua_buf[0:HALO, :] = jnp.zeros((HALO, ua_buf.shape[1]), F32)
        xc_buf[0:HALO, :] = jnp.zeros((HALO, xc_buf.shape[1]), F32)
        h_carry[...] = jnp.zeros_like(h_carry)

    ua_buf[HALO:HALO + ts, :] = ca_ref[...].astype(F32) * xa_ref[...].astype(F32)
    conv_a = None
    for k in range(ka):
        term = wa_ref[k:k + 1, :] * ua_buf[HALO - (ka - 1) + k:HALO - (ka - 1) + k + ts, :]
        conv_a = term if conv_a is None else conv_a + term
    ya_ref[...] = (ba_ref[...].astype(F32) * conv_a).astype(ya_ref.dtype)
    ua_buf[0:HALO, :] = ua_buf[ts:ts + HALO, :]

    xc_buf[HALO:HALO + ts, :] = xc_ref[...].astype(F32)
    xcv = None
    for k in range(kc):
        term = wc_ref[k:k + 1, :] * xc_buf[HALO - (kc - 1) + k:HALO - (kc - 1) + k + ts, :]
        xcv = term if xcv is None else xcv + term
    xcv = xcv + bc_ref[...]
    xc_buf[0:HALO, :] = xc_buf[ts:ts + HALO, :]

    xb = xcv.astype(BF16)
    r = jax.nn.sigmoid(jnp.dot(xb, wr_ref[...], preferred_element_type=F32) + br_ref[...])
    ig = jax.nn.sigmoid(jnp.dot(xb, wi_ref[...], preferred_element_type=F32) + bi_ref[...])
    lam = lam_ref[...]
    softplus_neg_lam = jnp.maximum(-lam, 0.0) + jnp.log1p(jnp.exp(-jnp.abs(lam)))
    log_a = (-LRU_C) * r * softplus_neg_lam
    a = jnp.exp(log_a)
    u = jnp.sqrt(-jnp.tanh(log_a) * (1.0 + a * a)) * (ig * xcv)

    row = lax.broadcasted_iota(jnp.int32, a.shape, 0)
    d = 1
    while d < ts:
        a_sh = jnp.where(row >= d, pltpu.roll(a, d, 0), 1.0)
        u_sh = jnp.where(row >= d, pltpu.roll(u, d, 0), 0.0)
        u = a * u_sh + u
        a = a * a_sh
        d *= 2
    h = a * h_carry[...] + u
    h_carry[...] = h[ts - 1:ts, :]

    gc = gc_ref[...].astype(F32)
    gelu = 0.5 * gc * (1.0 + jnp.tanh(math.sqrt(2.0 / math.pi) * (gc + 0.044715 * (gc * gc * gc))))
    yc_ref[...] = (gelu * h).astype(yc_ref.dtype)


def _seqmix(p3, wa, wc, bc, wr_bd, br, wi_bd, bi, lam, ts=256):
    b, s, _ = p3.shape
    cw = wa.shape[1]
    col = lambda c: pl.BlockSpec((None, ts, cw), lambda bi_, si: (bi_, si, c))
    full = lambda arr: pl.BlockSpec(arr.shape, lambda bi_, si: (0,) * arr.ndim)
    out = jax.ShapeDtypeStruct((b, s, cw), BF16)
    ospec = pl.BlockSpec((None, ts, cw), lambda bi_, si: (bi_, si, 0))
    return pl.pallas_call(
        _seqmix_kernel,
        grid=(b, s // ts),
        in_specs=[col(0), col(1), col(2), col(6), col(7),
                  full(wa), full(wc), full(bc), full(wr_bd), full(br), full(wi_bd), full(bi), full(lam)],
        out_specs=[ospec, ospec],
        out_shape=[out, out],
        scratch_shapes=[pltpu.VMEM((HALO + ts, cw), F32), pltpu.VMEM((HALO + ts, cw), F32),
                        pltpu.VMEM((1, cw), F32)],
        compiler_params=_cparams(("parallel", "arbitrary")),
        name="seqmix",
    )(p3, p3, p3, p3, p3, wa, wc, bc, wr_bd, br, wi_bd, bi, lam)


def _dot_nt(a, b):
    return lax.dot_general(a, b, (((1,), (1,)), ((), ())), preferred_element_type=F32)


def _attn_kernel(q_ref, k_ref, v_ref, cos_ref, sa_ref, sb_ref, qg_ref, kg_ref, o_ref,
                 qr_scr, kaug0, kaug1, wg_hi, wg_lo):
    s_len = k_ref.shape[0]
    blk = MOBA_BLOCK
    nb = s_len // blk
    i = pl.program_id(2)

    @pl.when(i == 0)
    def _prep():
        li = lax.broadcasted_iota(jnp.int32, (LANES, LANES), 0) // HEAD_DIM
        lj = lax.broadcasted_iota(jnp.int32, (LANES, LANES), 1) // HEAD_DIM
        head_ones = jnp.where(li == lj, 1.0, 0.0).astype(BF16)

        def norm_rot(t_ref, gain):
            t = t_ref[...].astype(F32)
            hi, lo = _split_bf16(t * t)
            ss = (jnp.dot(hi, head_ones, preferred_element_type=F32)
                  + jnp.dot(lo, head_ones, preferred_element_type=F32))
            tn = t * lax.rsqrt(ss * (1.0 / HEAD_DIM) + EPS) * gain
            return (tn * cos_ref[...] + pltpu.roll(tn, LANES - ROT_DIM // 2, 1) * sa_ref[...]
                    + pltpu.roll(tn, ROT_DIM // 2, 1) * sb_ref[...])

        qr_scr[...] = norm_rot(q_ref, qg_ref[...]) * (HEAD_DIM ** -0.5)
        kr = norm_rot(k_ref, kg_ref[...])
        lane = lax.broadcasted_iota(jnp.int32, (s_len, LANES), 1)
        kblk = lax.broadcasted_iota(jnp.int32, (s_len, LANES), 0) // blk
        kaug0[...] = jnp.where(lane < HEAD_DIM, kr,
                               jnp.where(lane - HEAD_DIM == kblk, 1.0, 0.0)).astype(BF16)
        kaug1[...] = jnp.where(lane >= HEAD_DIM, kr,
                               jnp.where(lane == kblk, 1.0, 0.0)).astype(BF16)
        kmean = jnp.sum(kr.reshape(nb, blk, LANES), axis=1) * (1.0 / blk)
        ml = lax.broadcasted_iota(jnp.int32, (nb, LANES), 1)
        km0 = jnp.where(ml < HEAD_DIM, kmean, 0.0)
        km1 = jnp.where(ml >= HEAD_DIM, kmean, 0.0)
        pad = jnp.zeros((HEAD_DIM - nb, LANES), F32)
        wg = jnp.concatenate([km1, pad, km0, pad], axis=0)
        hi, lo = _split_bf16(wg)
        wg_hi[...] = hi
        wg_lo[...] = lo

    row0 = pl.multiple_of(i * blk, blk)
    qf = qr_scr[pl.ds(row0, blk), :]
    q_hi, q_lo = _split_bf16(qf)
    gate = _dot_nt(q_hi, wg_hi[...]) + _dot_nt(q_hi, wg_lo[...]) + _dot_nt(q_lo, wg_hi[...])
    lane = lax.broadcasted_iota(jnp.int32, (blk, LANES), 1)

    def select_bias(lo):
        n = lane - lo
        g = jnp.where((n >= 0) & (n < i), gate, -jnp.inf)
        bias = jnp.where(n == i, 0.0, NEG)
        for k in range(MOBA_TOPK):
            m = jnp.max(g, axis=1, keepdims=True)
            idx = jnp.min(jnp.where(g == m, lane, 1 << 30), axis=1, keepdims=True)
            hit = lane == idx
            bias = jnp.where(hit & (k < i), 0.0, bias)
            g = jnp.where(hit, -jnp.inf, g)
        return bias

    qa0 = jnp.where(lane < HEAD_DIM, qf, select_bias(HEAD_DIM)).astype(BF16)
    qa1 = jnp.where(lane >= HEAD_DIM, qf, select_bias(0)).astype(BF16)

    def update(carry, qa, kj, vj, causal):
        m, l, acc = carry
        s = _dot_nt(qa, kj)
        if causal:
            r = lax.broadcasted_iota(jnp.int32, s.shape, 0)
            c = lax.broadcasted_iota(jnp.int32, s.shape, 1)
            s = jnp.where(c <= r, s, NEG)
        m_new = jnp.maximum(m, jnp.max(s, axis=1, keepdims=True))
        alpha = jnp.exp(m - m_new)
        p = jnp.exp(s - m_new)
        l = alpha * l + jnp.sum(p, axis=1, keepdims=True)
        acc = alpha * acc + jnp.dot(p.astype(BF16), vj, preferred_element_type=F32)
        return m_new, l, acc

    def init():
        return (jnp.full((blk, 1), NEG, F32), jnp.zeros((blk, 1), F32), jnp.zeros((blk, LANES), F32))

    def body(j, carry):
        c0, c1 = carry
        r = pl.multiple_of(j * blk, blk)
        vj = v_ref[pl.ds(r, blk), :]
        c0 = update(c0, qa0, kaug0[pl.ds(r, blk), :], vj, False)
        c1 = update(c1, qa1, kaug1[pl.ds(r, blk), :], vj, False)
        return c0, c1

    c0, c1 = lax.fori_loop(0, i, body, (init(), init()))
    vi = v_ref[pl.ds(row0, blk), :]
    _, l0, acc0 = update(c0, qa0, kaug0[pl.ds(row0, blk), :], vi, True)
    _, l1, acc1 = update(c1, qa1, kaug1[pl.ds(row0, blk), :], vi, True)
    o_ref[...] = jnp.where(lane < HEAD_DIM, acc0 / l0, acc1 / l1).astype(o_ref.dtype)


def _attention(p3, cos_t, sa_t, sb_t, qg, kg):
    b, s, _ = p3.shape
    n_tiles = N_HEADS // HEADS_PER_TILE
    nq = s // MOBA_BLOCK
    qcol, kcol, vcol = 12, 16, 20
    seq = lambda c0: pl.BlockSpec((None, s, LANES), lambda bi, hp, qi: (bi, 0, c0 + hp))
    tab = pl.BlockSpec((s, LANES), lambda bi, hp, qi: (0, 0))
    gain = pl.BlockSpec((1, LANES), lambda bi, hp, qi: (0, 0))
    return pl.pallas_call(
        _attn_kernel,
        grid=(b, n_tiles, nq),
        in_specs=[seq(qcol), seq(kcol), seq(vcol), tab, tab, tab, gain, gain],
        out_specs=pl.BlockSpec((None, MOBA_BLOCK, LANES), lambda bi, hp, qi: (bi, qi, hp)),
        out_shape=jax.ShapeDtypeStruct((b, s, N_HEADS * HEAD_DIM), BF16),
        scratch_shapes=[pltpu.VMEM((s, LANES), F32), pltpu.VMEM((s, LANES), BF16),
                        pltpu.VMEM((s, LANES), BF16), pltpu.VMEM((LANES, LANES), BF16),
                        pltpu.VMEM((LANES, LANES), BF16)],
        compiler_params=_cparams(("parallel", "parallel", "arbitrary")),
        name="moba_attention",
    )(p3, p3, p3, cos_t, sa_t, sb_t, qg, kg)


def _rotary_tables(s):
    half = ROT_DIM // 2
    inv_freq = ROPE_THETA ** (-jnp.arange(0, ROT_DIM, 2, dtype=F32) / ROT_DIM)
    ang = jnp.arange(s, dtype=jnp.int32).astype(F32)[:, None] * inv_freq[None, :]
    cos, sin = jnp.cos(ang), jnp.sin(ang)
    ones = jnp.ones((s, HEAD_DIM - ROT_DIM), F32)
    zeros = jnp.zeros((s, HEAD_DIM - ROT_DIM), F32)
    zh = jnp.zeros((s, half), F32)
    cos_h = jnp.concatenate([cos, cos, ones], axis=1)
    sa_h = jnp.concatenate([-sin, zh, zeros], axis=1)
    sb_h = jnp.concatenate([zh, sin, zeros], axis=1)
    tile = lambda t: jnp.tile(t, (1, HEADS_PER_TILE))
    return tile(cos_h), tile(sa_h), tile(sb_h)


def _route(logits, n_experts):
    lane = lax.broadcasted_iota(jnp.int32, logits.shape, 1)
    g = jnp.where(lane < n_experts, logits, -jnp.inf)
    m1 = jnp.max(g, axis=1, keepdims=True)
    i1 = jnp.min(jnp.where(g == m1, lane, 1 << 30), axis=1, keepdims=True)
    g2 = jnp.where(lane == i1, -jnp.inf, g)
    m2 = jnp.max(g2, axis=1, keepdims=True)
    i2 = jnp.min(jnp.where(g2 == m2, lane, 1 << 30), axis=1, keepdims=True)
    e2 = jnp.exp(m2 - m1)
    w1 = 1.0 / (1.0 + e2)
    w2 = e2 / (1.0 + e2)
    return jnp.where(lane == i1, w1, 0.0) + jnp.where(lane == i2, w2, 0.0)


def _merge_kernel(*refs, with_router, n_experts):
    if with_router:
        (x_ref, ya_ref, yb_ref, yc_ref, ga_ref, gb_ref, gcg_ref, woa_ref, wob_ref, woc_ref, wo_ref,
         gn_ref, rhi_ref, rlo_ref, x1_ref, h2_ref, comb_ref) = refs
    else:
        (x_ref, ya_ref, yb_ref, yc_ref, ga_ref, gb_ref, gcg_ref, woa_ref, wob_ref, woc_ref, wo_ref,
         gn_ref, x1_ref, h2_ref) = refs
    merged = (jax.nn.sigmoid(ga_ref[...].astype(F32))
              * jnp.dot(ya_ref[...], woa_ref[...], preferred_element_type=F32)
              + jax.nn.sigmoid(gb_ref[...].astype(F32))
              * jnp.dot(yb_ref[...], wob_ref[...], preferred_element_type=F32)
              + jax.nn.sigmoid(gcg_ref[...].astype(F32))
              * jnp.dot(yc_ref[...], woc_ref[...], preferred_element_type=F32))
    x1 = x_ref[...] + jnp.dot(merged.astype(BF16), wo_ref[...], preferred_element_type=F32)
    x1_ref[...] = x1
    h2 = _rms(x1, gn_ref[...])
    h2_ref[...] = h2.astype(BF16)
    if with_router:
        h_hi, h_lo = _split_bf16(h2)
        logits = (jnp.dot(h_hi, rhi_ref[...], preferred_element_type=F32)
                  + jnp.dot(h_hi, rlo_ref[...], preferred_element_type=F32)
                  + jnp.dot(h_lo, rhi_ref[...], preferred_element_type=F32))
        comb_ref[...] = _route(logits, n_experts)


def _merge(x2, ya, yb, yc, p2, woa, wob, woc, wo, gn, router=None, tm=512):
    t, d = x2.shape
    cw = ya.shape[1]
    row = lambda w: pl.BlockSpec((tm, w), lambda i: (i, 0))
    full = lambda arr: pl.BlockSpec(arr.shape, lambda i: (0,) * arr.ndim)
    gate_col0 = (p2.shape[1] - N_BRANCHES * d) // d
    gspec = lambda c: pl.BlockSpec((tm, d), lambda i: (i, gate_col0 + c))
    in_specs = [row(d), row(cw), row(cw), row(cw), gspec(0), gspec(1), gspec(2),
                full(woa), full(wob), full(woc), full(wo), full(gn)]
    args = [x2, ya, yb, yc, p2, p2, p2, woa, wob, woc, wo, gn]
    out_specs = [row(d), row(d)]
    out_shape = [jax.ShapeDtypeStruct((t, d), F32), jax.ShapeDtypeStruct((t, d), BF16)]
    n_experts = 0
    if router is not None:
        n_experts = router.shape[1]
        rpad = jnp.zeros((d, LANES), F32).at[:, :n_experts].set(router)
        r_hi, r_lo = _split_bf16(rpad)
        in_specs += [full(r_hi), full(r_lo)]
        args += [r_hi, r_lo]
        out_specs.append(row(LANES))
        out_shape.append(jax.ShapeDtypeStruct((t, LANES), F32))
    return pl.pallas_call(
        functools.partial(_merge_kernel, with_router=router is not None, n_experts=n_experts),
        grid=(t // tm,),
        in_specs=in_specs, out_specs=out_specs, out_shape=out_shape,
        compiler_params=_cparams(("parallel",)),
        name="merge",
    )(*args)


def _ffn_kernel(h_ref, x_ref, wg_ref, wu_ref, wd_ref, o_ref, acc):
    f = pl.program_id(1)
    h = h_ref[...]
    g = jnp.dot(h, wg_ref[...], preferred_element_type=F32)
    u = jnp.dot(h, wu_ref[...], preferred_element_type=F32)
    a = (g * jax.nn.sigmoid(g) * u).astype(BF16)
    part = jnp.dot(a, wd_ref[...], preferred_element_type=F32)

    @pl.when(f == 0)
    def _():
        acc[...] = x_ref[...] + part

    @pl.when(f > 0)
    def _():
        acc[...] += part

    @pl.when(f == pl.num_programs(1) - 1)
    def _():
        o_ref[...] = acc[...]


def _ffn(h2, x1, wg, wu, wd, tm=512, tf=1408):
    t, d = x1.shape
    ff = wg.shape[1]
    return pl.pallas_call(
        _ffn_kernel,
        grid=(t // tm, ff // tf),
        in_specs=[pl.BlockSpec((tm, d), lambda i, f: (i, 0)),
                  pl.BlockSpec((tm, d), lambda i, f: (i, 0)),
                  pl.BlockSpec((d, tf), lambda i, f: (0, f)),
                  pl.BlockSpec((d, tf), lambda i, f: (0, f)),
                  pl.BlockSpec((tf, d), lambda i, f: (f, 0))],
        out_specs=pl.BlockSpec((tm, d), lambda i, f: (i, 0)),
        out_shape=jax.ShapeDtypeStruct((t, d), F32),
        scratch_shapes=[pltpu.VMEM((tm, d), F32)],
        compiler_params=_cparams(("parallel", "arbitrary")),
        name="dense_ffn",
    )(h2, x1, wg, wu, wd)


def _moe_kernel(h_ref, x_ref, comb_ref, wg_ref, wu_ref, wd_ref, o_ref, acc):
    e = pl.program_id(1)
    f = pl.program_id(2)
    h = h_ref[...]
    g = jnp.dot(h, wg_ref[...], preferred_element_type=F32)
    u = jnp.dot(h, wu_ref[...], preferred_element_type=F32)
    a = (g * jax.nn.sigmoid(g) * u).astype(BF16)
    comb = comb_ref[...]
    lane = lax.broadcasted_iota(jnp.int32, comb.shape, 1)
    c_e = jnp.sum(jnp.where(lane == e, comb, 0.0), axis=1, keepdims=True)
    part = c_e * jnp.dot(a, wd_ref[...], preferred_element_type=F32)

    @pl.when((e == 0) & (f == 0))
    def _():
        acc[...] = x_ref[...] + part

    @pl.when((e > 0) | (f > 0))
    def _():
        acc[...] += part

    @pl.when((e == pl.num_programs(1) - 1) & (f == pl.num_programs(2) - 1))
    def _():
        o_ref[...] = acc[...]


def _moe(h2, x1, comb, wg, wu, wd, tm=1024, tf=896):
    t, d = x1.shape
    ne, _, ff = wg.shape
    return pl.pallas_call(
        _moe_kernel,
        grid=(t // tm, ne, ff // tf),
        in_specs=[pl.BlockSpec((tm, d), lambda i, e, f: (i, 0)),
                  pl.BlockSpec((tm, d), lambda i, e, f: (i, 0)),
                  pl.BlockSpec((tm, LANES), lambda i, e, f: (i, 0)),
                  pl.BlockSpec((None, d, tf), lambda i, e, f: (e, 0, f)),
                  pl.BlockSpec((None, d, tf), lambda i, e, f: (e, 0, f)),
                  pl.BlockSpec((None, tf, d), lambda i, e, f: (e, f, 0))],
        out_specs=pl.BlockSpec((tm, d), lambda i, e, f: (i, 0)),
        out_shape=jax.ShapeDtypeStruct((t, d), F32),
        scratch_shapes=[pltpu.VMEM((tm, d), F32)],
        compiler_params=_cparams(("parallel", "arbitrary", "arbitrary")),
        name="moe_ffn",
    )(h2, x1, comb, wg, wu, wd)


def _block_diag(w):
    nb, n, _ = w.shape
    eye = jnp.eye(nb, dtype=w.dtype)
    return (eye[:, None, :, None] * w[:, :, None, :]).reshape(nb * n, nb * n)


def kernel(x, norm_mix, w_in, conv_a_w, wo_a, q_norm, k_norm, wo_b, conv_c_w, conv_c_b, rg_w_r, rg_b_r, rg_w_i, rg_b_i, rg_lambda, wo_c, w_o, norm_ffn, ffn_w_gate, ffn_w_up, ffn_w_down, moe_router, moe_w_gate, moe_w_up, moe_w_down):
    bsz, s, d = x.shape
    depth = w_in.shape[0]
    cos_t, sa_t, sb_t = _rotary_tables(s)
    x2 = x.reshape(bsz * s, d)
    for l in range(depth):
        p2 = _inproj(x2, norm_mix[l][None, :], w_in[l].astype(BF16))
        p3 = p2.reshape(bsz, s, p2.shape[1])
        ya, yc = _seqmix(p3, conv_a_w[l], conv_c_w[l], conv_c_b[l][None, :],
                         _block_diag(rg_w_r[l]).astype(BF16), rg_b_r[l][None, :],
                         _block_diag(rg_w_i[l]).astype(BF16), rg_b_i[l][None, :],
                         rg_lambda[l][None, :])
        yb = _attention(p3, cos_t, sa_t, sb_t,
                        jnp.tile(q_norm[l], HEADS_PER_TILE)[None, :],
                        jnp.tile(k_norm[l], HEADS_PER_TILE)[None, :])
        cw = ya.shape[2]
        args = (x2, ya.reshape(bsz * s, cw), yb.reshape(bsz * s, yb.shape[2]), yc.reshape(bsz * s, cw), p2,
                wo_a[l].astype(BF16), wo_b[l].astype(BF16), wo_c[l].astype(BF16), w_o[l].astype(BF16),
                norm_ffn[l][None, :])
        if l % 2 == 0:
            x1, h2 = _merge(*args)
            x2 = _ffn(h2, x1, ffn_w_gate[l // 2].astype(BF16), ffn_w_up[l // 2].astype(BF16),
                      ffn_w_down[l // 2].astype(BF16))
        else:
            x1, h2, comb = _merge(*args, router=moe_router[l // 2])
            x2 = _moe(h2, x1, comb, moe_w_gate[l // 2].astype(BF16), moe_w_up[l // 2].astype(BF16),
                      moe_w_down[l // 2].astype(BF16))
    return x2.reshape(bsz, s, d)
```

```python
import functools
import math

import jax
import jax.numpy as jnp
from jax import lax
from jax.experimental import pallas as pl
from jax.experimental.pallas import tpu as pltpu

F32 = jnp.float32
BF16 = jnp.bfloat16

N_HEADS = 8
HEAD_DIM = 64
ROT_DIM = HEAD_DIM // 4
ROPE_THETA = 500000.0
MOBA_BLOCK = 256
MOBA_TOPK = 3
LRU_C = 8.0
N_BRANCHES = 3
TOP_K = 2
EPS = 1e-6
NEG = -1e30

LANES = 128
HEADS_PER_TILE = LANES // HEAD_DIM
ATTN_CHUNK = 4
VMEM_LIMIT = 56 * 1024 * 1024


def _cparams(sem):
    return pltpu.CompilerParams(dimension_semantics=sem, vmem_limit_bytes=VMEM_LIMIT)


def _split_bf16(x):
    hi = x.astype(BF16)
    lo = (x - hi.astype(F32)).astype(BF16)
    return hi, lo


def _rms(x, g):
    ms = jnp.mean(x * x, axis=-1, keepdims=True)
    return x * lax.rsqrt(ms + EPS) * g


def _inproj_kernel(x_ref, g_ref, w_ref, o_ref, h_scr):
    @pl.when(pl.program_id(1) == 0)
    def _():
        h_scr[...] = _rms(x_ref[...], g_ref[...]).astype(BF16)

    o_ref[...] = jnp.dot(h_scr[...], w_ref[...], preferred_element_type=F32).astype(o_ref.dtype)


def _inproj(x2, g, w, tm=1024, tn=1024):
    t, d = x2.shape
    n = w.shape[1]
    return pl.pallas_call(
        _inproj_kernel,
        grid=(t // tm, n // tn),
        in_specs=[pl.BlockSpec((tm, d), lambda i, j: (i, 0)),
                  pl.BlockSpec((1, d), lambda i, j: (0, 0)),
                  pl.BlockSpec((d, tn), lambda i, j: (0, j))],
        out_specs=pl.BlockSpec((tm, tn), lambda i, j: (i, j)),
        out_shape=jax.ShapeDtypeStruct((t, n), BF16),
        scratch_shapes=[pltpu.VMEM((tm, d), BF16)],
        compiler_params=_cparams(("parallel", "arbitrary")),
        name="inproj",
    )(x2, g, w)


HALO = 8


def _seqmix_kernel(xa_ref, ba_ref, ca_ref, xc_ref, gc_ref, wa_ref, wc_ref, bc_ref,
                   wr_ref, br_ref, wi_ref, bi_ref, lam_ref, ya_ref, yc_ref,
                   ua_buf, xc_buf, h_carry):
    ts = xa_ref.shape[0]
    ka = wa_ref.shape[0]
    kc = wc_ref.shape[0]

    @pl.when(pl.program_id(1) == 0)
    def _():
        ua_buf[0:HALO, :] = jnp.zeros((HALO, ua_buf.shape[1]), F32)
        xc_buf[0:HALO, :] = jnp.zeros((HALO, xc_buf.shape[1]), F32)
        h_carry[...] = jnp.zeros_like(h_carry)

    ua_buf[HALO:HALO + ts, :] = ca_ref[...].astype(F32) * xa_ref[...].astype(F32)
    conv_a = None
    for k in range(ka):
        term = wa_ref[k:k + 1, :] * ua_buf[HALO - (ka - 1) + k:HALO - (ka - 1) + k + ts, :]
        conv_a = term if conv_a is None else conv_a + term
    ya_ref[...] = (ba_ref[...].astype(F32) * conv_a).astype(ya_ref.dtype)
    ua_buf[0:HALO, :] = ua_buf[ts:ts + HALO, :]

    xc_buf[HALO:HALO + ts, :] = xc_ref[...].astype(F32)
    xcv = None
    for k in range(kc):
        term = wc_ref[k:k + 1, :] * xc_buf[HALO - (kc - 1) + k:HALO - (kc - 1) + k + ts, :]
        xcv = term if xcv is None else xcv + term
    xcv = xcv + bc_ref[...]
    xc_buf[0:HALO, :] = xc_buf[ts:ts + HALO, :]

    xb = xcv.astype(BF16)
    r = jax.nn.sigmoid(jnp.dot(xb, wr_ref[...], preferred_element_type=F32) + br_ref[...])
    ig = jax.nn.sigmoid(jnp.dot(xb, wi_ref[...], preferred_element_type=F32) + bi_ref[...])
    lam = lam_ref[...]
    softplus_neg_lam = jnp.maximum(-lam, 0.0) + jnp.log1p(jnp.exp(-jnp.abs(lam)))
    log_a = (-LRU_C) * r * softplus_neg_lam
    a = jnp.exp(log_a)
    u = jnp.sqrt(-jnp.tanh(log_a) * (1.0 + a * a)) * (ig * xcv)

    row = lax.broadcasted_iota(jnp.int32, a.shape, 0)
    d = 1
    while d < ts:
        a_sh = jnp.where(row >= d, pltpu.roll(a, d, 0), 1.0)
        u_sh = jnp.where(row >= d, pltpu.roll(u, d, 0), 0.0)
        u = a * u_sh + u
        a = a * a_sh
        d *= 2
    h = a * h_carry[...] + u
    h_carry[...] = h[ts - 1:ts, :]

    gc = gc_ref[...].astype(F32)
    gelu = 0.5 * gc * (1.0 + jnp.tanh(math.sqrt(2.0 / math.pi) * (gc + 0.044715 * (gc * gc * gc))))
    yc_ref[...] = (gelu * h).astype(yc_ref.dtype)


def _seqmix(p3, wa, wc, bc, wr_bd, br, wi_bd, bi, lam, ts=256):
    b, s, _ = p3.shape
    cw = wa.shape[1]
    col = lambda c: pl.BlockSpec((None, ts, cw), lambda bi_, si: (bi_, si, c))
    full = lambda arr: pl.BlockSpec(arr.shape, lambda bi_, si: (0,) * arr.ndim)
    out = jax.ShapeDtypeStruct((b, s, cw), BF16)
    ospec = pl.BlockSpec((None, ts, cw), lambda bi_, si: (bi_, si, 0))
    return pl.pallas_call(
        _seqmix_kernel,
        grid=(b, s // ts),
        in_specs=[col(0), col(1), col(2), col(6), col(7),
                  full(wa), full(wc), full(bc), full(wr_bd), full(br), full(wi_bd), full(bi), full(lam)],
        out_specs=[ospec, ospec],
        out_shape=[out, out],
        scratch_shapes=[pltpu.VMEM((HALO + ts, cw), F32), pltpu.VMEM((HALO + ts, cw), F32),
                        pltpu.VMEM((1, cw), F32)],
        compiler_params=_cparams(("parallel", "arbitrary")),
        name="seqmix",
    )(p3, p3, p3, p3, p3, wa, wc, bc, wr_bd, br, wi_bd, bi, lam)


def _dot_nt(a, b):
    return lax.dot_general(a, b, (((1,), (1,)), ((), ())), preferred_element_type=F32)


def _attn_kernel(q_ref, k_ref, v_ref, cos_ref, sa_ref, sb_ref, qg_ref, kg_ref, o_ref,
                 qa_scr, kaug, vaug, s_scr, mx_scr, m_scr, acc_scr):
    s_len = k_ref.shape[0]
    blk = MOBA_BLOCK
    nb = s_len // blk
    i = pl.program_id(2)

    @pl.when(i == 0)
    def _prep():
        li = lax.broadcasted_iota(jnp.int32, (LANES, LANES), 0) // HEAD_DIM
        lj = lax.broadcasted_iota(jnp.int32, (LANES, LANES), 1) // HEAD_DIM
        head_ones = jnp.where(li == lj, 1.0, 0.0).astype(BF16)

        def norm_rot(t_ref, gain):
            t = t_ref[...].astype(F32)
            hi, lo = _split_bf16(t * t)
            ss = (jnp.dot(hi, head_ones, preferred_element_type=F32)
                  + jnp.dot(lo, head_ones, preferred_element_type=F32))
            tn = t * lax.rsqrt(ss * (1.0 / HEAD_DIM) + EPS) * gain
            return (tn * cos_ref[...] + pltpu.roll(tn, LANES - ROT_DIM // 2, 1) * sa_ref[...]
                    + pltpu.roll(tn, ROT_DIM // 2, 1) * sb_ref[...])

        qr = norm_rot(q_ref, qg_ref[...]) * (HEAD_DIM ** -0.5)
        kr = norm_rot(k_ref, kg_ref[...])
        lane = lax.broadcasted_iota(jnp.int32, (s_len, LANES), 1)
        kblk = lax.broadcasted_iota(jnp.int32, (s_len, LANES), 0) // blk
        kaug[0] = jnp.where(lane < HEAD_DIM, kr,
                            jnp.where(lane - HEAD_DIM == kblk, 1.0, 0.0)).astype(BF16)
        kaug[1] = jnp.where(lane >= HEAD_DIM, kr,
                            jnp.where(lane == kblk, 1.0, 0.0)).astype(BF16)
        v = v_ref[...]
        vaug[0] = jnp.where(lane < HEAD_DIM, v, jnp.ones_like(v))
        vaug[1] = jnp.where(lane >= HEAD_DIM, v, jnp.ones_like(v))
        kmean = jnp.sum(kr.reshape(nb, blk, LANES), axis=1) * (1.0 / blk)
        ml = lax.broadcasted_iota(jnp.int32, (nb, LANES), 1)
        km0 = jnp.where(ml < HEAD_DIM, kmean, 0.0)
        km1 = jnp.where(ml >= HEAD_DIM, kmean, 0.0)
        pad = jnp.zeros((HEAD_DIM - nb, LANES), F32)
        wg = jnp.concatenate([km1, pad, km0, pad], axis=0)
        w_hi, w_lo = _split_bf16(wg)
        q_hi, q_lo = _split_bf16(qr)
        gate_t = _dot_nt(w_hi, q_hi) + _dot_nt(w_lo, q_hi) + _dot_nt(w_hi, q_lo)
        n = lax.broadcasted_iota(jnp.int32, (nb, s_len), 0)
        qblk = lax.broadcasted_iota(jnp.int32, (nb, s_len), 1) // blk

        def select_bias_t(g_rows):
            g = jnp.where(n < qblk, g_rows, -jnp.inf)
            bias = jnp.where(n == qblk, 0.0, NEG)
            for k in range(MOBA_TOPK):
                m = jnp.max(g, axis=0, keepdims=True)
                idx = jnp.min(jnp.where(g == m, n, 1 << 30), axis=0, keepdims=True)
                hit = n == idx
                bias = jnp.where(hit & (k < qblk), 0.0, bias)
                g = jnp.where(hit, -jnp.inf, g)
            return bias

        pad_t = jnp.zeros((HEAD_DIM - nb, s_len), F32)
        bias_t = jnp.concatenate([select_bias_t(gate_t[0:nb]), pad_t,
                                  select_bias_t(gate_t[HEAD_DIM:HEAD_DIM + nb]), pad_t], axis=0)
        bias = bias_t.T
        qa_scr[0] = jnp.where(lane < HEAD_DIM, qr, bias).astype(BF16)
        qa_scr[1] = jnp.where(lane >= HEAD_DIM, qr, bias).astype(BF16)

    row0 = pl.multiple_of(i * blk, blk)
    lane = lax.broadcasted_iota(jnp.int32, (blk, LANES), 1)
    mx_scr[...] = jnp.full(mx_scr.shape, -jnp.inf, F32)

    n_full = i // ATTN_CHUNK
    cw = ATTN_CHUNK * blk

    def lane_tile_max(s):
        out = s[:, 0:LANES]
        for t in range(1, cw // LANES):
            out = jnp.maximum(out, s[:, t * LANES:(t + 1) * LANES])
        return out

    def scores(c, h, causal):
        r = pl.multiple_of(c * cw, cw)
        s = _dot_nt(qa_scr[h, pl.ds(row0, blk), :], kaug[h, pl.ds(r, cw), :])
        if causal:
            delta = (lax.broadcasted_iota(jnp.int32, s.shape, 1)
                     - lax.broadcasted_iota(jnp.int32, s.shape, 0))
            s = jnp.where(delta <= (i - c * ATTN_CHUNK) * blk, s, NEG)
        s_scr[h, c] = s
        mx_scr[h] = jnp.maximum(mx_scr[h], lane_tile_max(s))

    def pass1(c, carry):
        for h in range(HEADS_PER_TILE):
            scores(c, h, False)
        return carry

    lax.fori_loop(0, n_full, pass1, 0)
    for h in range(HEADS_PER_TILE):
        scores(n_full, h, True)
        m_scr[h] = jnp.broadcast_to(jnp.max(mx_scr[h], axis=1, keepdims=True), (blk, LANES))
    acc_scr[...] = jnp.zeros(acc_scr.shape, F32)

    def pass2(c, carry):
        r = pl.multiple_of(c * cw, cw)
        for h in range(HEADS_PER_TILE):
            mb = jnp.concatenate([m_scr[h]] * (cw // LANES), axis=1)
            p = jnp.exp(s_scr[h, c] - mb).astype(BF16)
            acc_scr[h] += jnp.dot(p, vaug[h, pl.ds(r, cw), :], preferred_element_type=F32)
        return carry

    lax.fori_loop(0, n_full + 1, pass2, 0)
    a0 = acc_scr[0]
    a1 = acc_scr[1]
    o_ref[...] = jnp.where(lane < HEAD_DIM, a0 / pltpu.roll(a0, HEAD_DIM, 1),
                           a1 / pltpu.roll(a1, HEAD_DIM, 1)).astype(o_ref.dtype)


def _attention(p3, cos_t, sa_t, sb_t, qg, kg):
    b, s, _ = p3.shape
    n_tiles = N_HEADS // HEADS_PER_TILE
    nq = s // MOBA_BLOCK
    qcol, kcol, vcol = 12, 16, 20
    seq = lambda c0: pl.BlockSpec((None, s, LANES), lambda bi, hp, qi: (bi, 0, c0 + hp))
    tab = pl.BlockSpec((s, LANES), lambda bi, hp, qi: (0, 0))
    gain = pl.BlockSpec((1, LANES), lambda bi, hp, qi: (0, 0))
    return pl.pallas_call(
        _attn_kernel,
        grid=(b, n_tiles, nq),
        in_specs=[seq(qcol), seq(kcol), seq(vcol), tab, tab, tab, gain, gain],
        out_specs=pl.BlockSpec((None, MOBA_BLOCK, LANES), lambda bi, hp, qi: (bi, qi, hp)),
        out_shape=jax.ShapeDtypeStruct((b, s, N_HEADS * HEAD_DIM), BF16),
        scratch_shapes=[pltpu.VMEM((HEADS_PER_TILE, s, LANES), BF16),
                        pltpu.VMEM((HEADS_PER_TILE, s, LANES), BF16),
                        pltpu.VMEM((HEADS_PER_TILE, s, LANES), BF16),
                        pltpu.VMEM((HEADS_PER_TILE, nq // ATTN_CHUNK, MOBA_BLOCK, ATTN_CHUNK * MOBA_BLOCK), F32),
                        pltpu.VMEM((HEADS_PER_TILE, MOBA_BLOCK, LANES), F32),
                        pltpu.VMEM((HEADS_PER_TILE, MOBA_BLOCK, LANES), F32),
                        pltpu.VMEM((HEADS_PER_TILE, MOBA_BLOCK, LANES), F32)],
        compiler_params=_cparams(("parallel", "parallel", "arbitrary")),
        name="moba_attention",
    )(p3, p3, p3, cos_t, sa_t, sb_t, qg, kg)


def _rotary_tables(s):
    half = ROT_DIM // 2
    inv_freq = ROPE_THETA ** (-jnp.arange(0, ROT_DIM, 2, dtype=F32) / ROT_DIM)
    ang = jnp.arange(s, dtype=jnp.int32).astype(F32)[:, None] * inv_freq[None, :]
    cos, sin = jnp.cos(ang), jnp.sin(ang)
    ones = jnp.ones((s, HEAD_DIM - ROT_DIM), F32)
    zeros = jnp.zeros((s, HEAD_DIM - ROT_DIM), F32)
    zh = jnp.zeros((s, half), F32)
    cos_h = jnp.concatenate([cos, cos, ones], axis=1)
    sa_h = jnp.concatenate([-sin, zh, zeros], axis=1)
    sb_h = jnp.concatenate([zh, sin, zeros], axis=1)
    tile = lambda t: jnp.tile(t, (1, HEADS_PER_TILE))
    return tile(cos_h), tile(sa_h), tile(sb_h)


def _route(logits, n_experts):
    lane = lax.broadcasted_iota(jnp.int32, logits.shape, 1)
    g = jnp.where(lane < n_experts, logits, -jnp.inf)
    m1 = jnp.max(g, axis=1, keepdims=True)
    i1 = jnp.min(jnp.where(g == m1, lane, 1 << 30), axis=1, keepdims=True)
    g2 = jnp.where(lane == i1, -jnp.inf, g)
    m2 = jnp.max(g2, axis=1, keepdims=True)
    i2 = jnp.min(jnp.where(g2 == m2, lane, 1 << 30), axis=1, keepdims=True)
    e2 = jnp.exp(m2 - m1)
    w1 = 1.0 / (1.0 + e2)
    w2 = e2 / (1.0 + e2)
    return jnp.where(lane == 0, w1, jnp.where(lane == 1, w2, jnp.where(
        lane == 2, i1.astype(F32), jnp.where(lane == 3, i2.astype(F32), 0.0))))


def _merge_kernel(*refs, with_router, n_experts):
    if with_router:
        (x_ref, ya_ref, yb_ref, yc_ref, ga_ref, gb_ref, gcg_ref, woa_ref, wob_ref, woc_ref, wo_ref,
         gn_ref, rhi_ref, rlo_ref, x1_ref, h2_ref, route_ref) = refs
    else:
        (x_ref, ya_ref, yb_ref, yc_ref, ga_ref, gb_ref, gcg_ref, woa_ref, wob_ref, woc_ref, wo_ref,
         gn_ref, x1_ref, h2_ref) = refs
    merged = (jax.nn.sigmoid(ga_ref[...].astype(F32))
              * jnp.dot(ya_ref[...], woa_ref[...], preferred_element_type=F32)
              + jax.nn.sigmoid(gb_ref[...].astype(F32))
              * jnp.dot(yb_ref[...], wob_ref[...], preferred_element_type=F32)
              + jax.nn.sigmoid(gcg_ref[...].astype(F32))
              * jnp.dot(yc_ref[...], woc_ref[...], preferred_element_type=F32))
    x1 = x_ref[...] + jnp.dot(merged.astype(BF16), wo_ref[...], preferred_element_type=F32)
    x1_ref[...] = x1
    h2 = _rms(x1, gn_ref[...])
    h2_ref[...] = h2.astype(h2_ref.dtype)
    if with_router:
        h_hi, h_lo = _split_bf16(h2)
        logits = (jnp.dot(h_hi, rhi_ref[...], preferred_element_type=F32)
                  + jnp.dot(h_hi, rlo_ref[...], preferred_element_type=F32)
                  + jnp.dot(h_lo, rhi_ref[...], preferred_element_type=F32))
        route_ref[...] = _route(logits, n_experts)


def _merge(x2, ya, yb, yc, p2, woa, wob, woc, wo, gn, router=None, tm=512):
    t, d = x2.shape
    cw = ya.shape[1]
    row = lambda w: pl.BlockSpec((tm, w), lambda i: (i, 0))
    full = lambda arr: pl.BlockSpec(arr.shape, lambda i: (0,) * arr.ndim)
    gate_col0 = (p2.shape[1] - N_BRANCHES * d) // d
    gspec = lambda c: pl.BlockSpec((tm, d), lambda i: (i, gate_col0 + c))
    in_specs = [row(d), row(cw), row(cw), row(cw), gspec(0), gspec(1), gspec(2),
                full(woa), full(wob), full(woc), full(wo), full(gn)]
    args = [x2, ya, yb, yc, p2, p2, p2, woa, wob, woc, wo, gn]
    out_specs = [row(d), row(d)]
    out_shape = [jax.ShapeDtypeStruct((t, d), F32),
                 jax.ShapeDtypeStruct((t, d), BF16 if router is None else F32)]
    n_experts = 0
    if router is not None:
        n_experts = router.shape[1]
        rpad = jnp.zeros((d, LANES), F32).at[:, :n_experts].set(router)
        r_hi, r_lo = _split_bf16(rpad)
        in_specs += [full(r_hi), full(r_lo)]
        args += [r_hi, r_lo]
        out_specs.append(row(LANES))
        out_shape.append(jax.ShapeDtypeStruct((t, LANES), F32))
    return pl.pallas_call(
        functools.partial(_merge_kernel, with_router=router is not None, n_experts=n_experts),
        grid=(t // tm,),
        in_specs=in_specs, out_specs=out_specs, out_shape=out_shape,
        compiler_params=_cparams(("parallel",)),
        name="merge",
    )(*args)


def _ffn_kernel(h_ref, x_ref, wg_ref, wu_ref, wd_ref, o_ref, acc):
    f = pl.program_id(1)
    h = h_ref[...]
    g = jnp.dot(h, wg_ref[...], preferred_element_type=F32)
    u = jnp.dot(h, wu_ref[...], preferred_element_type=F32)
    a = (g * jax.nn.sigmoid(g) * u).astype(BF16)
    part = jnp.dot(a, wd_ref[...], preferred_element_type=F32)

    @pl.when(f == 0)
    def _():
        acc[...] = x_ref[...] + part

    @pl.when(f > 0)
    def _():
        acc[...] += part

    @pl.when(f == pl.num_programs(1) - 1)
    def _():
        o_ref[...] = acc[...]


def _ffn(h2, x1, wg, wu, wd, tm=512, tf=1408):
    t, d = x1.shape
    ff = wg.shape[1]
    return pl.pallas_call(
        _ffn_kernel,
        grid=(t // tm, ff // tf),
        in_specs=[pl.BlockSpec((tm, d), lambda i, f: (i, 0)),
                  pl.BlockSpec((tm, d), lambda i, f: (i, 0)),
                  pl.BlockSpec((d, tf), lambda i, f: (0, f)),
                  pl.BlockSpec((d, tf), lambda i, f: (0, f)),
                  pl.BlockSpec((tf, d), lambda i, f: (f, 0))],
        out_specs=pl.BlockSpec((tm, d), lambda i, f: (i, 0)),
        out_shape=jax.ShapeDtypeStruct((t, d), F32),
        scratch_shapes=[pltpu.VMEM((tm, d), F32)],
        compiler_params=_cparams(("parallel", "arbitrary")),
        name="dense_ffn",
    )(h2, x1, wg, wu, wd)


def _moe_plan(route, n_experts, tm):
    t = route.shape[0]
    ids = route[:, 2:2 + TOP_K].astype(jnp.int32).reshape(-1)
    onehot = (ids[:, None] == jnp.arange(n_experts, dtype=jnp.int32)[None, :]).astype(jnp.int32)
    csum = jnp.cumsum(onehot, axis=0)
    rank = jnp.sum((csum - onehot) * onehot, axis=1)
    counts = csum[-1]
    gsize = (counts + tm - 1) // tm * tm
    gend = jnp.cumsum(gsize)
    gstart = gend - gsize
    dest = gstart[ids] + rank
    n_tiles = (TOP_K * t) // tm + n_experts
    n_rows = n_tiles * tm
    pair = jnp.full((n_rows,), -1, jnp.int32).at[dest].set(jnp.arange(TOP_K * t, dtype=jnp.int32))
    row = jnp.arange(n_rows, dtype=jnp.int32)
    tok = jnp.where(pair >= 0, pair // TOP_K, 0)
    ydst = jnp.where(pair >= 0, (pair % TOP_K) * t + pair // TOP_K, TOP_K * t + row % tm)
    tile_start = jnp.arange(n_tiles, dtype=jnp.int32) * tm
    tile_valid = (tile_start < gend[-1]).astype(jnp.int32)
    tile_expert = jnp.minimum(jnp.searchsorted(gend, tile_start, side="right"), n_experts - 1).astype(jnp.int32)
    last_expert = tile_expert[jnp.maximum(jnp.sum(tile_valid) - 1, 0)]
    tile_expert = jnp.where(tile_valid == 1, tile_expert, last_expert)
    return tok.reshape(n_tiles, tm), ydst.reshape(n_tiles, tm), tile_expert, tile_valid


def _moe_ffn_kernel(te_ref, tv_ref, tok_hbm, ydst_hbm, h_hbm, wg_ref, wu_ref, wd_ref, y_hbm,
                    x_buf, x_bf, acc, tok_smem, ydst_smem, sem_idx, sem_gather, sem_scatter):
    r = pl.program_id(0)
    f = pl.program_id(1)
    tm = x_buf.shape[0]
    valid = tv_ref[r] == 1

    @pl.when((r == 0) & (f == 0))
    def _init_spare_rows():
        acc[...] = jnp.zeros(acc.shape, F32)
        spare = pltpu.make_async_copy(acc, y_hbm.at[pl.ds(y_hbm.shape[0] - tm, tm), :], sem_scatter)
        spare.start()
        spare.wait()

    @pl.when(valid & (f == 0))
    def _gather():
        c_tok = pltpu.make_async_copy(tok_hbm.at[r], tok_smem, sem_idx.at[0])
        c_dst = pltpu.make_async_copy(ydst_hbm.at[r], ydst_smem, sem_idx.at[1])
        c_tok.start()
        c_dst.start()
        c_tok.wait()
        c_dst.wait()

        def issue(k, carry):
            pltpu.make_async_copy(h_hbm.at[pl.ds(tok_smem[k], 1), :], x_buf.at[pl.ds(k, 1), :],
                                  sem_gather).start()
            return carry

        lax.fori_loop(0, tm, issue, 0, unroll=8)
        pltpu.make_async_copy(h_hbm.at[pl.ds(0, tm), :], x_buf, sem_gather).wait()
        x_bf[...] = x_buf[...].astype(BF16)

    @pl.when(valid)
    def _compute():
        x = x_bf[...]
        g = jnp.dot(x, wg_ref[...], preferred_element_type=F32)
        u = jnp.dot(x, wu_ref[...], preferred_element_type=F32)
        a = (g * jax.nn.sigmoid(g) * u).astype(BF16)
        part = jnp.dot(a, wd_ref[...], preferred_element_type=F32)

        @pl.when(f == 0)
        def _():
            acc[...] = part

        @pl.when(f > 0)
        def _():
            acc[...] += part

    @pl.when(valid & (f == pl.num_programs(1) - 1))
    def _scatter():
        def issue(k, carry):
            pltpu.make_async_copy(acc.at[pl.ds(k, 1), :], y_hbm.at[pl.ds(ydst_smem[k], 1), :],
                                  sem_scatter).start()
            return carry

        lax.fori_loop(0, tm, issue, 0, unroll=8)
        pltpu.make_async_copy(acc, y_hbm.at[pl.ds(0, tm), :], sem_scatter).wait()


def _moe_ffn(h2, tok, ydst, tile_expert, tile_valid, wg, wu, wd, tf=512):
    t, d = h2.shape
    n_tiles, tm = tok.shape
    ff = wg.shape[2]
    nf = ff // tf
    fsel = lambda f, tv, r: jnp.where(tv[r] == 1, f, nf - 1)
    grid_spec = pltpu.PrefetchScalarGridSpec(
        num_scalar_prefetch=2,
        grid=(n_tiles, nf),
        in_specs=[pl.BlockSpec(memory_space=pl.ANY), pl.BlockSpec(memory_space=pl.ANY),
                  pl.BlockSpec(memory_space=pl.ANY),
                  pl.BlockSpec((None, d, tf), lambda r, f, te, tv: (te[r], 0, fsel(f, tv, r))),
                  pl.BlockSpec((None, d, tf), lambda r, f, te, tv: (te[r], 0, fsel(f, tv, r))),
                  pl.BlockSpec((None, tf, d), lambda r, f, te, tv: (te[r], fsel(f, tv, r), 0))],
        out_specs=pl.BlockSpec(memory_space=pl.ANY),
        scratch_shapes=[pltpu.VMEM((tm, d), F32), pltpu.VMEM((tm, d), BF16), pltpu.VMEM((tm, d), F32),
                        pltpu.SMEM((tm,), jnp.int32), pltpu.SMEM((tm,), jnp.int32),
                        pltpu.SemaphoreType.DMA((2,)), pltpu.SemaphoreType.DMA, pltpu.SemaphoreType.DMA],
    )
    return pl.pallas_call(
        _moe_ffn_kernel,
        grid_spec=grid_spec,
        out_shape=jax.ShapeDtypeStruct((TOP_K * t + tm, d), F32),
        compiler_params=_cparams(("arbitrary", "arbitrary")),
        name="moe_ffn",
    )(tile_expert, tile_valid, tok, ydst, h2, wg, wu, wd)


def _combine_kernel(x_ref, y0_ref, y1_ref, route_ref, o_ref):
    route = route_ref[...]
    lane = lax.broadcasted_iota(jnp.int32, route.shape, 1)
    w0 = jnp.sum(jnp.where(lane == 0, route, 0.0), axis=1, keepdims=True)
    w1 = jnp.sum(jnp.where(lane == 1, route, 0.0), axis=1, keepdims=True)
    o_ref[...] = x_ref[...] + w0 * y0_ref[...] + w1 * y1_ref[...]


def _combine(x1, y, route, tm=512):
    t, d = x1.shape
    nblk = t // tm
    return pl.pallas_call(
        _combine_kernel,
        grid=(nblk,),
        in_specs=[pl.BlockSpec((tm, d), lambda i: (i, 0)),
                  pl.BlockSpec((tm, d), lambda i: (i, 0)),
                  pl.BlockSpec((tm, d), lambda i: (i + nblk, 0)),
                  pl.BlockSpec((tm, LANES), lambda i: (i, 0))],
        out_specs=pl.BlockSpec((tm, d), lambda i: (i, 0)),
        out_shape=jax.ShapeDtypeStruct((t, d), F32),
        compiler_params=_cparams(("parallel",)),
        name="moe_combine",
    )(x1, y, y, route)


def _moe(h2, x1, route, wg, wu, wd, tm=512):
    tok, ydst, tile_expert, tile_valid = _moe_plan(route, wg.shape[0], tm)
    y = _moe_ffn(h2, tok, ydst, tile_expert, tile_valid, wg, wu, wd)
    return _combine(x1, y, route)


def _block_diag(w):
    nb, n, _ = w.shape
    eye = jnp.eye(nb, dtype=w.dtype)
    return (eye[:, None, :, None] * w[:, :, None, :]).reshape(nb * n, nb * n)


def kernel(x, norm_mix, w_in, conv_a_w, wo_a, q_norm, k_norm, wo_b, conv_c_w, conv_c_b, rg_w_r, rg_b_r, rg_w_i, rg_b_i, rg_lambda, wo_c, w_o, norm_ffn, ffn_w_gate, ffn_w_up, ffn_w_down, moe_router, moe_w_gate, moe_w_up, moe_w_down):
    bsz, s, d = x.shape
    depth = w_in.shape[0]
    cos_t, sa_t, sb_t = _rotary_tables(s)
    x2 = x.reshape(bsz * s, d)
    for l in range(depth):
        p2 = _inproj(x2, norm_mix[l][None, :], w_in[l].astype(BF16))
        p3 = p2.reshape(bsz, s, p2.shape[1])
        ya, yc = _seqmix(p3, conv_a_w[l], conv_c_w[l], conv_c_b[l][None, :],
                         _block_diag(rg_w_r[l]).astype(BF16), rg_b_r[l][None, :],
                         _block_diag(rg_w_i[l]).astype(BF16), rg_b_i[l][None, :],
                         rg_lambda[l][None, :])
        yb = _attention(p3, cos_t, sa_t, sb_t,
                        jnp.tile(q_norm[l], HEADS_PER_TILE)[None, :],
                        jnp.tile(k_norm[l], HEADS_PER_TILE)[None, :])
        cw = ya.shape[2]
        args = (x2, ya.reshape(bsz * s, cw), yb.reshape(bsz * s, yb.shape[2]), yc.reshape(bsz * s, cw), p2,
                wo_a[l].astype(BF16), wo_b[l].astype(BF16), wo_c[l].astype(BF16), w_o[l].astype(BF16),
                norm_ffn[l][None, :])
        if l % 2 == 0:
            x1, h2 = _merge(*args)
            x2 = _ffn(h2, x1, ffn_w_gate[l // 2].astype(BF16), ffn_w_up[l // 2].astype(BF16),
                      ffn_w_down[l // 2].astype(BF16))
        else:
            x1, h2, route = _merge(*args, router=moe_router[l // 2])
            x2 = _moe(h2, x1, route, moe_w_gate[l // 2].astype(BF16), moe_w_up[l // 2].astype(BF16),
                      moe_w_down[l // 2].astype(BF16))
    return x2.reshape(bsz, s, d)
```

```python
import functools
import math

import jax
import jax.numpy as jnp
from jax import lax
from jax.experimental import pallas as pl
from jax.experimental.pallas import tpu as pltpu

F32 = jnp.float32
BF16 = jnp.bfloat16

N_HEADS = 8
HEAD_DIM = 64
ROT_DIM = HEAD_DIM // 4
ROPE_THETA = 500000.0
MOBA_BLOCK = 256
MOBA_TOPK = 3
LRU_C = 8.0
N_BRANCHES = 3
TOP_K = 2
EPS = 1e-6
NEG = -1e30

LANES = 128
HEADS_PER_TILE = LANES // HEAD_DIM
ATTN_CHUNK = 4
VMEM_LIMIT = 56 * 1024 * 1024


def _cparams(sem):
    return pltpu.CompilerParams(dimension_semantics=sem, vmem_limit_bytes=VMEM_LIMIT)


def _split_bf16(x):
    hi = x.astype(BF16)
    lo = (x - hi.astype(F32)).astype(BF16)
    return hi, lo


def _rms(x, g):
    ms = jnp.mean(x * x, axis=-1, keepdims=True)
    return x * lax.rsqrt(ms + EPS) * g


def _inproj_kernel(x_ref, g_ref, w_ref, o_ref, h_scr):
    @pl.when(pl.program_id(1) == 0)
    def _():
        h_scr[...] = _rms(x_ref[...], g_ref[...]).astype(BF16)

    o_ref[...] = jnp.dot(h_scr[...], w_ref[...], preferred_element_type=F32).astype(o_ref.dtype)


def _inproj(x2, g, w, tm=1024, tn=1024):
    t, d = x2.shape
    n = w.shape[1]
    return pl.pallas_call(
        _inproj_kernel,
        grid=(t // tm, n // tn),
        in_specs=[pl.BlockSpec((tm, d), lambda i, j: (i, 0)),
                  pl.BlockSpec((1, d), lambda i, j: (0, 0)),
                  pl.BlockSpec((d, tn), lambda i, j: (0, j))],
        out_specs=pl.BlockSpec((tm, tn), lambda i, j: (i, j)),
        out_shape=jax.ShapeDtypeStruct((t, n), BF16),
        scratch_shapes=[pltpu.VMEM((tm, d), BF16)],
        compiler_params=_cparams(("parallel", "arbitrary")),
        name="inproj",
    )(x2, g, w)


HALO = 8


def _seqmix_kernel(xa_ref, ba_ref, ca_ref, xc_ref, gc_ref, wa_ref, wc_ref, bc_ref,
                   wr_ref, br_ref, wi_ref, bi_ref, lam_ref, ya_ref, yc_ref,
                   ua_buf, xc_buf, h_carry):
    ts = xa_ref.shape[0]
    ka = wa_ref.shape[0]
    kc = wc_ref.shape[0]

    @pl.when(pl.program_id(1) == 0)
    def _():
        ua_buf[0:HALO, :] = jnp.zeros((HALO, ua_buf.shape[1]), F32)
        xc_buf[0:HALO, :] = jnp.zeros((HALO, xc_buf.shape[1]), F32)
        h_carry[...] = jnp.zeros_like(h_carry)

    ua_buf[HALO:HALO + ts, :] = ca_ref[...].astype(F32) * xa_ref[...].astype(F32)
    conv_a = None
    for k in range(ka):
        term = wa_ref[k:k + 1, :] * ua_buf[HALO - (ka - 1) + k:HALO - (ka - 1) + k + ts, :]
        conv_a = term if conv_a is None else conv_a + term
    ya_ref[...] = (ba_ref[...].astype(F32) * conv_a).astype(ya_ref.dtype)
    ua_buf[0:HALO, :] = ua_buf[ts:ts + HALO, :]

    xc_buf[HALO:HALO + ts, :] = xc_ref[...].astype(F32)
    xcv = None
    for k in range(kc):
        term = wc_ref[k:k + 1, :] * xc_buf[HALO - (kc - 1) + k:HALO - (kc - 1) + k + ts, :]
        xcv = term if xcv is None else xcv + term
    xcv = xcv + bc_ref[...]
    xc_buf[0:HALO, :] = xc_buf[ts:ts + HALO, :]

    xb = xcv.astype(BF16)
    r = jax.nn.sigmoid(jnp.dot(xb, wr_ref[...], preferred_element_type=F32) + br_ref[...])
    ig = jax.nn.sigmoid(jnp.dot(xb, wi_ref[...], preferred_element_type=F32) + bi_ref[...])
    lam = lam_ref[...]
    softplus_neg_lam = jnp.maximum(-lam, 0.0) + jnp.log1p(jnp.exp(-jnp.abs(lam)))
    log_a = (-LRU_C) * r * softplus_neg_lam
    a = jnp.exp(log_a)
    u = jnp.sqrt(-jnp.tanh(log_a) * (1.0 + a * a)) * (ig * xcv)

    row = lax.broadcasted_iota(jnp.int32, a.shape, 0)
    d = 1
    while d < ts:
        a_sh = jnp.where(row >= d, pltpu.roll(a, d, 0), 1.0)
        u_sh = jnp.where(row >= d, pltpu.roll(u, d, 0), 0.0)
        u = a * u_sh + u
        a = a * a_sh
        d *= 2
    h = a * h_carry[...] + u
    h_carry[...] = h[ts - 1:ts, :]

    gc = gc_ref[...].astype(F32)
    gelu = 0.5 * gc * (1.0 + jnp.tanh(math.sqrt(2.0 / math.pi) * (gc + 0.044715 * (gc * gc * gc))))
    yc_ref[...] = (gelu * h).astype(yc_ref.dtype)


def _seqmix(p3, wa, wc, bc, wr_bd, br, wi_bd, bi, lam, ts=256):
    b, s, _ = p3.shape
    cw = wa.shape[1]
    col = lambda c: pl.BlockSpec((None, ts, cw), lambda bi_, si: (bi_, si, c))
    full = lambda arr: pl.BlockSpec(arr.shape, lambda bi_, si: (0,) * arr.ndim)
    out = jax.ShapeDtypeStruct((b, s, cw), BF16)
    ospec = pl.BlockSpec((None, ts, cw), lambda bi_, si: (bi_, si, 0))
    return pl.pallas_call(
        _seqmix_kernel,
        grid=(b, s // ts),
        in_specs=[col(0), col(1), col(2), col(6), col(7),
                  full(wa), full(wc), full(bc), full(wr_bd), full(br), full(wi_bd), full(bi), full(lam)],
        out_specs=[ospec, ospec],
        out_shape=[out, out],
        scratch_shapes=[pltpu.VMEM((HALO + ts, cw), F32), pltpu.VMEM((HALO + ts, cw), F32),
                        pltpu.VMEM((1, cw), F32)],
        compiler_params=_cparams(("parallel", "arbitrary")),
        name="seqmix",
    )(p3, p3, p3, p3, p3, wa, wc, bc, wr_bd, br, wi_bd, bi, lam)


def _dot_nt(a, b):
    return lax.dot_general(a, b, (((1,), (1,)), ((), ())), preferred_element_type=F32)


def _attn_kernel(q_ref, k_ref, v_ref, cos_ref, sa_ref, sb_ref, qg_ref, kg_ref, o_ref,
                 qa_scr, kaug, vaug, s_scr, mx_scr, m_scr, acc_scr):
    s_len = k_ref.shape[0]
    blk = MOBA_BLOCK
    nb = s_len // blk
    i = pl.program_id(2)

    @pl.when(i == 0)
    def _prep():
        li = lax.broadcasted_iota(jnp.int32, (LANES, LANES), 0) // HEAD_DIM
        lj = lax.broadcasted_iota(jnp.int32, (LANES, LANES), 1) // HEAD_DIM
        head_ones = jnp.where(li == lj, 1.0, 0.0).astype(BF16)

        def norm_rot(t_ref, gain):
            t = t_ref[...].astype(F32)
            hi, lo = _split_bf16(t * t)
            ss = (jnp.dot(hi, head_ones, preferred_element_type=F32)
                  + jnp.dot(lo, head_ones, preferred_element_type=F32))
            tn = t * lax.rsqrt(ss * (1.0 / HEAD_DIM) + EPS) * gain
            return (tn * cos_ref[...] + pltpu.roll(tn, LANES - ROT_DIM // 2, 1) * sa_ref[...]
                    + pltpu.roll(tn, ROT_DIM // 2, 1) * sb_ref[...])

        qr = norm_rot(q_ref, qg_ref[...]) * (HEAD_DIM ** -0.5 * math.log2(math.e))
        kr = norm_rot(k_ref, kg_ref[...])
        lane = lax.broadcasted_iota(jnp.int32, (s_len, LANES), 1)
        kblk = lax.broadcasted_iota(jnp.int32, (s_len, LANES), 0) // blk
        kaug[0] = jnp.where(lane < HEAD_DIM, kr,
                            jnp.where(lane - HEAD_DIM == kblk, 1.0, 0.0)).astype(BF16)
        kaug[1] = jnp.where(lane >= HEAD_DIM, kr,
                            jnp.where(lane == kblk, 1.0, 0.0)).astype(BF16)
        v = v_ref[...]
        vaug[0] = jnp.where(lane < HEAD_DIM, v, jnp.ones_like(v))
        vaug[1] = jnp.where(lane >= HEAD_DIM, v, jnp.ones_like(v))
        kmean = jnp.sum(kr.reshape(nb, blk, LANES), axis=1) * (1.0 / blk)
        ml = lax.broadcasted_iota(jnp.int32, (nb, LANES), 1)
        km0 = jnp.where(ml < HEAD_DIM, kmean, 0.0)
        km1 = jnp.where(ml >= HEAD_DIM, kmean, 0.0)
        pad = jnp.zeros((HEAD_DIM - nb, LANES), F32)
        wg = jnp.concatenate([km1, pad, km0, pad], axis=0)
        w_hi, w_lo = _split_bf16(wg)
        q_hi, q_lo = _split_bf16(qr)
        gate_t = _dot_nt(w_hi, q_hi) + _dot_nt(w_lo, q_hi) + _dot_nt(w_hi, q_lo)
        n = lax.broadcasted_iota(jnp.int32, (nb, s_len), 0)
        qblk = lax.broadcasted_iota(jnp.int32, (nb, s_len), 1) // blk

        def select_bias_t(g_rows):
            g = jnp.where(n < qblk, g_rows, -jnp.inf)
            bias = jnp.where(n == qblk, 0.0, NEG)
            for k in range(MOBA_TOPK):
                m = jnp.max(g, axis=0, keepdims=True)
                idx = jnp.min(jnp.where(g == m, n, 1 << 30), axis=0, keepdims=True)
                hit = n == idx
                bias = jnp.where(hit & (k < qblk), 0.0, bias)
                g = jnp.where(hit, -jnp.inf, g)
            return bias

        pad_t = jnp.zeros((HEAD_DIM - nb, s_len), F32)
        bias_t = jnp.concatenate([select_bias_t(gate_t[0:nb]), pad_t,
                                  select_bias_t(gate_t[HEAD_DIM:HEAD_DIM + nb]), pad_t], axis=0)
        bias = bias_t.T
        qa_scr[0] = jnp.where(lane < HEAD_DIM, qr, bias).astype(BF16)
        qa_scr[1] = jnp.where(lane >= HEAD_DIM, qr, bias).astype(BF16)

    row0 = pl.multiple_of(i * blk, blk)
    lane = lax.broadcasted_iota(jnp.int32, (blk, LANES), 1)
    mx_scr[...] = jnp.full(mx_scr.shape, -jnp.inf, F32)

    n_full = i // ATTN_CHUNK
    cw = ATTN_CHUNK * blk

    def keep_scores(c, h, s, width):
        s_scr[h, c, :, 0:width] = s
        out = s[:, 0:LANES]
        for t in range(1, width // LANES):
            out = jnp.maximum(out, s[:, t * LANES:(t + 1) * LANES])
        mx_scr[h] = jnp.maximum(mx_scr[h], out)

    def pass1(c, carry):
        r = pl.multiple_of(c * cw, cw)
        for h in range(HEADS_PER_TILE):
            keep_scores(c, h, _dot_nt(qa_scr[h, pl.ds(row0, blk), :], kaug[h, pl.ds(r, cw), :]), cw)
        return carry

    lax.fori_loop(0, n_full, pass1, 0)

    rem = i - n_full * ATTN_CHUNK
    tail0 = pl.multiple_of(n_full * cw, cw)
    for nblk in range(1, ATTN_CHUNK + 1):
        @pl.when(rem == nblk - 1)
        def _tail_scores(nblk=nblk):
            tri = (lax.broadcasted_iota(jnp.int32, (blk, blk), 1)
                   <= lax.broadcasted_iota(jnp.int32, (blk, blk), 0))
            for h in range(HEADS_PER_TILE):
                s = _dot_nt(qa_scr[h, pl.ds(row0, blk), :], kaug[h, pl.ds(tail0, nblk * blk), :])
                own = jnp.where(tri, s[:, (nblk - 1) * blk:], NEG)
                s = own if nblk == 1 else jnp.concatenate([s[:, :(nblk - 1) * blk], own], axis=1)
                keep_scores(n_full, h, s, nblk * blk)

    for h in range(HEADS_PER_TILE):
        m_scr[h] = jnp.broadcast_to(jnp.max(mx_scr[h], axis=1, keepdims=True), (blk, LANES))
    acc_scr[...] = jnp.zeros(acc_scr.shape, F32)

    def weigh(c, h, r, width):
        mb = jnp.concatenate([m_scr[h]] * (width // LANES), axis=1)
        p = jnp.exp2(s_scr[h, c, :, 0:width] - mb).astype(BF16)
        acc_scr[h] += jnp.dot(p, vaug[h, pl.ds(r, width), :], preferred_element_type=F32)

    def pass2(c, carry):
        r = pl.multiple_of(c * cw, cw)
        for h in range(HEADS_PER_TILE):
            weigh(c, h, r, cw)
        return carry

    lax.fori_loop(0, n_full, pass2, 0)
    for nblk in range(1, ATTN_CHUNK + 1):
        @pl.when(rem == nblk - 1)
        def _tail_weigh(nblk=nblk):
            for h in range(HEADS_PER_TILE):
                weigh(n_full, h, tail0, nblk * blk)

    a0 = acc_scr[0]
    a1 = acc_scr[1]
    o_ref[...] = jnp.where(lane < HEAD_DIM, a0 / pltpu.roll(a0, HEAD_DIM, 1),
                           a1 / pltpu.roll(a1, HEAD_DIM, 1)).astype(o_ref.dtype)


def _attention(p3, cos_t, sa_t, sb_t, qg, kg):
    b, s, _ = p3.shape
    n_tiles = N_HEADS // HEADS_PER_TILE
    nq = s // MOBA_BLOCK
    qcol, kcol, vcol = 12, 16, 20
    seq = lambda c0: pl.BlockSpec((None, s, LANES), lambda bi, hp, qi: (bi, 0, c0 + hp))
    tab = pl.BlockSpec((s, LANES), lambda bi, hp, qi: (0, 0))
    gain = pl.BlockSpec((1, LANES), lambda bi, hp, qi: (0, 0))
    return pl.pallas_call(
        _attn_kernel,
        grid=(b, n_tiles, nq),
        in_specs=[seq(qcol), seq(kcol), seq(vcol), tab, tab, tab, gain, gain],
        out_specs=pl.BlockSpec((None, MOBA_BLOCK, LANES), lambda bi, hp, qi: (bi, qi, hp)),
        out_shape=jax.ShapeDtypeStruct((b, s, N_HEADS * HEAD_DIM), BF16),
        scratch_shapes=[pltpu.VMEM((HEADS_PER_TILE, s, LANES), BF16),
                        pltpu.VMEM((HEADS_PER_TILE, s, LANES), BF16),
                        pltpu.VMEM((HEADS_PER_TILE, s, LANES), BF16),
                        pltpu.VMEM((HEADS_PER_TILE, nq // ATTN_CHUNK, MOBA_BLOCK, ATTN_CHUNK * MOBA_BLOCK), F32),
                        pltpu.VMEM((HEADS_PER_TILE, MOBA_BLOCK, LANES), F32),
                        pltpu.VMEM((HEADS_PER_TILE, MOBA_BLOCK, LANES), F32),
                        pltpu.VMEM((HEADS_PER_TILE, MOBA_BLOCK, LANES), F32)],
        compiler_params=_cparams(("parallel", "parallel", "arbitrary")),
        name="moba_attention",
    )(p3, p3, p3, cos_t, sa_t, sb_t, qg, kg)


def _rotary_tables(s):
    half = ROT_DIM // 2
    inv_freq = ROPE_THETA ** (-jnp.arange(0, ROT_DIM, 2, dtype=F32) / ROT_DIM)
    ang = jnp.arange(s, dtype=jnp.int32).astype(F32)[:, None] * inv_freq[None, :]
    cos, sin = jnp.cos(ang), jnp.sin(ang)
    ones = jnp.ones((s, HEAD_DIM - ROT_DIM), F32)
    zeros = jnp.zeros((s, HEAD_DIM - ROT_DIM), F32)
    zh = jnp.zeros((s, half), F32)
    cos_h = jnp.concatenate([cos, cos, ones], axis=1)
    sa_h = jnp.concatenate([-sin, zh, zeros], axis=1)
    sb_h = jnp.concatenate([zh, sin, zeros], axis=1)
    tile = lambda t: jnp.tile(t, (1, HEADS_PER_TILE))
    return tile(cos_h), tile(sa_h), tile(sb_h)


def _route(logits, n_experts):
    lane = lax.broadcasted_iota(jnp.int32, logits.shape, 1)
    g = jnp.where(lane < n_experts, logits, -jnp.inf)
    m1 = jnp.max(g, axis=1, keepdims=True)
    i1 = jnp.min(jnp.where(g == m1, lane, 1 << 30), axis=1, keepdims=True)
    g2 = jnp.where(lane == i1, -jnp.inf, g)
    m2 = jnp.max(g2, axis=1, keepdims=True)
    i2 = jnp.min(jnp.where(g2 == m2, lane, 1 << 30), axis=1, keepdims=True)
    e2 = jnp.exp(m2 - m1)
    w1 = 1.0 / (1.0 + e2)
    w2 = e2 / (1.0 + e2)
    return jnp.where(lane == 0, w1, jnp.where(lane == 1, w2, jnp.where(
        lane == 2, i1.astype(F32), jnp.where(lane == 3, i2.astype(F32), 0.0))))


def _merge_kernel(*refs, with_router, n_experts):
    if with_router:
        (x_ref, ya_ref, yb_ref, yc_ref, ga_ref, gb_ref, gcg_ref, woa_ref, wob_ref, woc_ref, wo_ref,
         gn_ref, rhi_ref, rlo_ref, x1_ref, h2_ref, route_ref) = refs
    else:
        (x_ref, ya_ref, yb_ref, yc_ref, ga_ref, gb_ref, gcg_ref, woa_ref, wob_ref, woc_ref, wo_ref,
         gn_ref, x1_ref, h2_ref) = refs
    merged = (jax.nn.sigmoid(ga_ref[...].astype(F32))
              * jnp.dot(ya_ref[...], woa_ref[...], preferred_element_type=F32)
              + jax.nn.sigmoid(gb_ref[...].astype(F32))
              * jnp.dot(yb_ref[...], wob_ref[...], preferred_element_type=F32)
              + jax.nn.sigmoid(gcg_ref[...].astype(F32))
              * jnp.dot(yc_ref[...], woc_ref[...], preferred_element_type=F32))
    x1 = x_ref[...] + jnp.dot(merged.astype(BF16), wo_ref[...], preferred_element_type=F32)
    x1_ref[...] = x1
    h2 = _rms(x1, gn_ref[...])
    if with_router:
        n_lane_tiles = h2.shape[1] // LANES
        for j in range(n_lane_tiles):
            h2_ref[pl.ds(j, h2.shape[0], stride=n_lane_tiles), :] = h2[:, j * LANES:(j + 1) * LANES]
        h_hi, h_lo = _split_bf16(h2)
        logits = (jnp.dot(h_hi, rhi_ref[...], preferred_element_type=F32)
                  + jnp.dot(h_hi, rlo_ref[...], preferred_element_type=F32)
                  + jnp.dot(h_lo, rhi_ref[...], preferred_element_type=F32))
        route_ref[...] = _route(logits, n_experts)
    else:
        h2_ref[...] = h2.astype(BF16)


def _merge(x2, ya, yb, yc, p2, woa, wob, woc, wo, gn, router=None, tm=512):
    t, d = x2.shape
    cw = ya.shape[1]
    row = lambda w: pl.BlockSpec((tm, w), lambda i: (i, 0))
    full = lambda arr: pl.BlockSpec(arr.shape, lambda i: (0,) * arr.ndim)
    gate_col0 = (p2.shape[1] - N_BRANCHES * d) // d
    gspec = lambda c: pl.BlockSpec((tm, d), lambda i: (i, gate_col0 + c))
    in_specs = [row(d), row(cw), row(cw), row(cw), gspec(0), gspec(1), gspec(2),
                full(woa), full(wob), full(woc), full(wo), full(gn)]
    args = [x2, ya, yb, yc, p2, p2, p2, woa, wob, woc, wo, gn]
    out_specs = [row(d), row(d)]
    out_shape = [jax.ShapeDtypeStruct((t, d), F32), jax.ShapeDtypeStruct((t, d), BF16)]
    n_experts = 0
    if router is not None:
        n_experts = router.shape[1]
        rpad = jnp.zeros((d, LANES), F32).at[:, :n_experts].set(router)
        r_hi, r_lo = _split_bf16(rpad)
        in_specs += [full(r_hi), full(r_lo)]
        args += [r_hi, r_lo]
        out_specs[1] = pl.BlockSpec((tm * (d // LANES), LANES), lambda i: (i, 0))
        out_shape[1] = jax.ShapeDtypeStruct((t * (d // LANES), LANES), F32)
        out_specs.append(row(LANES))
        out_shape.append(jax.ShapeDtypeStruct((t, LANES), F32))
    return pl.pallas_call(
        functools.partial(_merge_kernel, with_router=router is not None, n_experts=n_experts),
        grid=(t // tm,),
        in_specs=in_specs, out_specs=out_specs, out_shape=out_shape,
        compiler_params=_cparams(("parallel",)),
        name="merge",
    )(*args)


def _ffn_kernel(h_ref, x_ref, wg_ref, wu_ref, wd_ref, o_ref, acc):
    f = pl.program_id(1)
    h = h_ref[...]
    g = jnp.dot(h, wg_ref[...], preferred_element_type=F32)
    u = jnp.dot(h, wu_ref[...], preferred_element_type=F32)
    a = (g * jax.nn.sigmoid(g) * u).astype(BF16)
    part = jnp.dot(a, wd_ref[...], preferred_element_type=F32)

    @pl.when(f == 0)
    def _():
        acc[...] = x_ref[...] + part

    @pl.when(f > 0)
    def _():
        acc[...] += part

    @pl.when(f == pl.num_programs(1) - 1)
    def _():
        o_ref[...] = acc[...]


def _ffn(h2, x1, wg, wu, wd, tm=512, tf=1408):
    t, d = x1.shape
    ff = wg.shape[1]
    return pl.pallas_call(
        _ffn_kernel,
        grid=(t // tm, ff // tf),
        in_specs=[pl.BlockSpec((tm, d), lambda i, f: (i, 0)),
                  pl.BlockSpec((tm, d), lambda i, f: (i, 0)),
                  pl.BlockSpec((d, tf), lambda i, f: (0, f)),
                  pl.BlockSpec((d, tf), lambda i, f: (0, f)),
                  pl.BlockSpec((tf, d), lambda i, f: (f, 0))],
        out_specs=pl.BlockSpec((tm, d), lambda i, f: (i, 0)),
        out_shape=jax.ShapeDtypeStruct((t, d), F32),
        scratch_shapes=[pltpu.VMEM((tm, d), F32)],
        compiler_params=_cparams(("parallel", "arbitrary")),
        name="dense_ffn",
    )(h2, x1, wg, wu, wd)


def _moe_plan(route, n_experts, tm):
    t = route.shape[0]
    ids = route[:, 2:2 + TOP_K].astype(jnp.int32).reshape(-1)
    onehot = (ids[:, None] == jnp.arange(n_experts, dtype=jnp.int32)[None, :]).astype(jnp.int32)
    csum = jnp.cumsum(onehot, axis=0)
    rank = jnp.sum((csum - onehot) * onehot, axis=1)
    counts = csum[-1]
    gsize = (counts + tm - 1) // tm * tm
    gend = jnp.cumsum(gsize)
    gstart = gend - gsize
    dest = gstart[ids] + rank
    n_tiles = (TOP_K * t) // tm + n_experts
    n_rows = n_tiles * tm
    pair = jnp.full((n_rows,), -1, jnp.int32).at[dest].set(jnp.arange(TOP_K * t, dtype=jnp.int32))
    row = jnp.arange(n_rows, dtype=jnp.int32)
    tok = jnp.where(pair >= 0, pair // TOP_K, 0)
    ydst = jnp.where(pair >= 0, (pair % TOP_K) * t + pair // TOP_K, TOP_K * t + row % tm)
    tile_start = jnp.arange(n_tiles, dtype=jnp.int32) * tm
    tile_valid = (tile_start < gend[-1]).astype(jnp.int32)
    tile_expert = jnp.minimum(jnp.sum((tile_start[:, None] >= gend[None, :]).astype(jnp.int32), axis=1),
                              n_experts - 1)
    last_expert = tile_expert[jnp.maximum(jnp.sum(tile_valid) - 1, 0)]
    tile_expert = jnp.where(tile_valid == 1, tile_expert, last_expert)
    spare = TOP_K * t + jnp.arange(tm, dtype=jnp.int32)
    ydst = jnp.concatenate([spare[None, :], ydst.reshape(n_tiles, tm)], axis=0)
    return tok.reshape(n_tiles, tm), ydst, tile_expert, tile_valid


def _moe_ffn_kernel(te_ref, tv_ref, tok_hbm, ydst_hbm, h_hbm, wg_ref, wu_ref, wd_ref, y_hbm,
                    x_buf, x_bf, acc, y_buf, tok_smem, ydst_smem, sem_idx, sem_gather, sem_scatter,
                    *, n_tiles, nf):
    r = pl.program_id(0)
    f = pl.program_id(1)
    tm = acc.shape[0]
    n_lane_tiles = acc.shape[1] // LANES
    share = tm // nf
    valid = tv_ref[r] == 1

    def idx_copies(tile):
        src_tile = jnp.minimum(tile, n_tiles - 1)
        return (pltpu.make_async_copy(tok_hbm.at[src_tile], tok_smem.at[tile & 3], sem_idx.at[0]),
                pltpu.make_async_copy(ydst_hbm.at[src_tile], ydst_smem.at[tile & 3], sem_idx.at[1]))

    def row_tile(buf, k):
        return buf.at[pl.ds(pl.multiple_of(k * n_lane_tiles, n_lane_tiles), n_lane_tiles), :]

    def gather_wait():
        pltpu.make_async_copy(x_buf, x_buf, sem_gather).wait()

    def scatter_wait():
        pltpu.make_async_copy(y_buf, y_buf, sem_scatter).wait()

    def start_gather(slot, k):
        pltpu.make_async_copy(h_hbm.at[tok_smem[slot, k]], row_tile(x_buf, k), sem_gather).start()

    def start_scatter(slot, k):
        pltpu.make_async_copy(row_tile(y_buf, k), y_hbm.at[ydst_smem[slot, k]], sem_scatter).start()

    def issue_share():
        nxt = (r + 1) & 3
        cur = r & 3
        for j in range(share):
            k = f * share + j
            start_gather(nxt, k)
            start_scatter(cur, k)

    @pl.when((r == 0) & (f == 0))
    def _first():
        acc[...] = jnp.zeros(acc.shape, F32)
        for c in idx_copies(0):
            c.start()
        for c in idx_copies(0):
            c.wait()

        def issue(k, carry):
            start_gather(0, k)
            return carry

        lax.fori_loop(0, tm, issue, 0, unroll=8)
        for c in idx_copies(1):
            c.start()

    @pl.when(f == 0)
    def _tile_start():
        @pl.when(r >= 1)
        def _():
            scatter_wait()

        for j in range(n_lane_tiles):
            y_buf[pl.ds(j, tm, stride=n_lane_tiles), :] = acc[:, j * LANES:(j + 1) * LANES]
        gather_wait()
        for j in range(n_lane_tiles):
            x_bf[:, j * LANES:(j + 1) * LANES] = x_buf[pl.ds(j, tm, stride=n_lane_tiles), :].astype(BF16)
        for c in idx_copies(r + 1):
            c.wait()
        for c in idx_copies(r + 2):
            c.start()

    @pl.when(valid)
    def _compute():
        issue_share()
        x = x_bf[...]
        g = jnp.dot(x, wg_ref[...], preferred_element_type=F32)
        u = jnp.dot(x, wu_ref[...], preferred_element_type=F32)
        a = (g * jax.nn.sigmoid(g) * u).astype(BF16)
        part = jnp.dot(a, wd_ref[...], preferred_element_type=F32)

        @pl.when(f == 0)
        def _():
            acc[...] = part

        @pl.when(f > 0)
        def _():
            acc[...] += part

    @pl.when(jnp.logical_not(valid))
    def _idle():
        issue_share()

    @pl.when((r == n_tiles - 1) & (f == nf - 1))
    def _drain():
        gather_wait()
        scatter_wait()
        for c in idx_copies(r + 2):
            c.wait()


def _moe_ffn(h2, tok, ydst, tile_expert, tile_valid, wg, wu, wd, tf=896):
    d = wg.shape[1]
    n_lane_tiles = d // LANES
    t = h2.shape[0] // n_lane_tiles
    h2 = h2.reshape(t, n_lane_tiles, LANES)
    n_tiles, tm = tok.shape
    ff = wg.shape[2]
    nf = ff // tf
    assert tm % nf == 0 and ydst.shape == (n_tiles + 1, tm)
    fsel = lambda f, tv, r: jnp.where(tv[r] == 1, f, nf - 1)
    grid_spec = pltpu.PrefetchScalarGridSpec(
        num_scalar_prefetch=2,
        grid=(n_tiles, nf),
        in_specs=[pl.BlockSpec(memory_space=pl.ANY), pl.BlockSpec(memory_space=pl.ANY),
                  pl.BlockSpec(memory_space=pl.ANY),
                  pl.BlockSpec((None, d, tf), lambda r, f, te, tv: (te[r], 0, fsel(f, tv, r))),
                  pl.BlockSpec((None, d, tf), lambda r, f, te, tv: (te[r], 0, fsel(f, tv, r))),
                  pl.BlockSpec((None, tf, d), lambda r, f, te, tv: (te[r], fsel(f, tv, r), 0))],
        out_specs=pl.BlockSpec(memory_space=pl.ANY),
        scratch_shapes=[pltpu.VMEM((tm * n_lane_tiles, LANES), F32), pltpu.VMEM((tm, d), BF16),
                        pltpu.VMEM((tm, d), F32), pltpu.VMEM((tm * n_lane_tiles, LANES), F32),
                        pltpu.SMEM((4, tm), jnp.int32), pltpu.SMEM((4, tm), jnp.int32),
                        pltpu.SemaphoreType.DMA((2,)), pltpu.SemaphoreType.DMA, pltpu.SemaphoreType.DMA],
    )
    return pl.pallas_call(
        functools.partial(_moe_ffn_kernel, n_tiles=n_tiles, nf=nf),
        grid_spec=grid_spec,
        out_shape=jax.ShapeDtypeStruct((TOP_K * t + tm, n_lane_tiles, LANES), F32),
        compiler_params=_cparams(("arbitrary", "arbitrary")),
        name="moe_ffn",
    )(tile_expert, tile_valid, tok, ydst, h2, wg, wu, wd)


def _combine_kernel(x_ref, y0_ref, y1_ref, route_ref, o_ref):
    route = route_ref[...]
    lane = lax.broadcasted_iota(jnp.int32, route.shape, 1)
    w0 = jnp.sum(jnp.where(lane == 0, route, 0.0), axis=1, keepdims=True)
    w1 = jnp.sum(jnp.where(lane == 1, route, 0.0), axis=1, keepdims=True)
    tm, d = x_ref.shape
    n_lane_tiles = d // LANES
    for j in range(n_lane_tiles):
        cols = slice(j * LANES, (j + 1) * LANES)
        rows = pl.ds(j, tm, stride=n_lane_tiles)
        o_ref[:, cols] = x_ref[:, cols] + w0 * y0_ref[rows, :] + w1 * y1_ref[rows, :]


def _combine(x1, y, route, tm=512):
    t, d = x1.shape
    nblk = t // tm
    return pl.pallas_call(
        _combine_kernel,
        grid=(nblk,),
        in_specs=[pl.BlockSpec((tm, d), lambda i: (i, 0)),
                  pl.BlockSpec((tm * (d // LANES), LANES), lambda i: (i, 0)),
                  pl.BlockSpec((tm * (d // LANES), LANES), lambda i: (i + nblk, 0)),
                  pl.BlockSpec((tm, LANES), lambda i: (i, 0))],
        out_specs=pl.BlockSpec((tm, d), lambda i: (i, 0)),
        out_shape=jax.ShapeDtypeStruct((t, d), F32),
        compiler_params=_cparams(("parallel",)),
        name="moe_combine",
    )(x1, y, y, route)


def _moe(h2, x1, route, wg, wu, wd, tm=512):
    tok, ydst, tile_expert, tile_valid = _moe_plan(route, wg.shape[0], tm)
    y = _moe_ffn(h2, tok, ydst, tile_expert, tile_valid, wg, wu, wd)
    return _combine(x1, y.reshape(-1, LANES), route)


def _block_diag(w):
    nb, n, _ = w.shape
    eye = jnp.eye(nb, dtype=w.dtype)
    return (eye[:, None, :, None] * w[:, :, None, :]).reshape(nb * n, nb * n)


def kernel(x, norm_mix, w_in, conv_a_w, wo_a, q_norm, k_norm, wo_b, conv_c_w, conv_c_b, rg_w_r, rg_b_r, rg_w_i, rg_b_i, rg_lambda, wo_c, w_o, norm_ffn, ffn_w_gate, ffn_w_up, ffn_w_down, moe_router, moe_w_gate, moe_w_up, moe_w_down):
    bsz, s, d = x.shape
    depth = w_in.shape[0]
    cos_t, sa_t, sb_t = _rotary_tables(s)
    x2 = x.reshape(bsz * s, d)
    for l in range(depth):
        p2 = _inproj(x2, norm_mix[l][None, :], w_in[l].astype(BF16))
        p3 = p2.reshape(bsz, s, p2.shape[1])
        ya, yc = _seqmix(p3, conv_a_w[l], conv_c_w[l], conv_c_b[l][None, :],
                         _block_diag(rg_w_r[l]).astype(BF16), rg_b_r[l][None, :],
                         _block_diag(rg_w_i[l]).astype(BF16), rg_b_i[l][None, :],
                         rg_lambda[l][None, :])
        yb = _attention(p3, cos_t, sa_t, sb_t,
                        jnp.tile(q_norm[l], HEADS_PER_TILE)[None, :],
                        jnp.tile(k_norm[l], HEADS_PER_TILE)[None, :])
        cw = ya.shape[2]
        args = (x2, ya.reshape(bsz * s, cw), yb.reshape(bsz * s, yb.shape[2]), yc.reshape(bsz * s, cw), p2,
                wo_a[l].astype(BF16), wo_b[l].astype(BF16), wo_c[l].astype(BF16), w_o[l].astype(BF16),
                norm_ffn[l][None, :])
        if l % 2 == 0:
            x1, h2 = _merge(*args)
            x2 = _ffn(h2, x1, ffn_w_gate[l // 2].astype(BF16), ffn_w_up[l // 2].astype(BF16),
                      ffn_w_down[l // 2].astype(BF16))
        else:
            x1, h2, route = _merge(*args, router=moe_router[l // 2])
            x2 = _moe(h2, x1, route, moe_w_gate[l // 2].astype(BF16), moe_w_up[l // 2].astype(BF16),
                      moe_w_down[l // 2].astype(BF16))
    return x2.reshape(bsz, s, d)
```

```python
import functools
import math

import jax
import jax.numpy as jnp
from jax import lax
from jax.experimental import pallas as pl
from jax.experimental.pallas import tpu as pltpu

F32 = jnp.float32
BF16 = jnp.bfloat16

N_HEADS = 8
HEAD_DIM = 64
ROT_DIM = HEAD_DIM // 4
ROPE_THETA = 500000.0
MOBA_BLOCK = 256
MOBA_TOPK = 3
LRU_C = 8.0
N_BRANCHES = 3
TOP_K = 2
EPS = 1e-6
NEG = -1e30

LANES = 128
HEADS_PER_TILE = LANES // HEAD_DIM
ATTN_CHUNK = 4
VMEM_LIMIT = 56 * 1024 * 1024


def _cparams(sem):
    return pltpu.CompilerParams(dimension_semantics=sem, vmem_limit_bytes=VMEM_LIMIT)


def _split_bf16(x):
    hi = x.astype(BF16)
    lo = (x - hi.astype(F32)).astype(BF16)
    return hi, lo


def _rms(x, g):
    ms = jnp.mean(x * x, axis=-1, keepdims=True)
    return x * lax.rsqrt(ms + EPS) * g


def _inproj_kernel(x_ref, g_ref, w_ref, o_ref, *, tn):
    h = _rms(x_ref[...], g_ref[...]).astype(BF16)
    for c in range(o_ref.shape[1] // tn):
        cols = slice(c * tn, (c + 1) * tn)
        o_ref[:, cols] = jnp.dot(h, w_ref[:, cols], preferred_element_type=F32).astype(o_ref.dtype)


def _inproj(x2, g, w, tm=512, tn=1024):
    t, d = x2.shape
    n = w.shape[1]
    return pl.pallas_call(
        functools.partial(_inproj_kernel, tn=tn),
        grid=(t // tm,),
        in_specs=[pl.BlockSpec((tm, d), lambda i: (i, 0)),
                  pl.BlockSpec((1, d), lambda i: (0, 0)),
                  pl.BlockSpec((d, n), lambda i: (0, 0), pipeline_mode=pl.Buffered(1))],
        out_specs=pl.BlockSpec((tm, n), lambda i: (i, 0)),
        out_shape=jax.ShapeDtypeStruct((t, n), BF16),
        compiler_params=_cparams(("parallel",)),
        name="inproj",
    )(x2, g, w)


HALO = 8


def _seqmix_kernel(xa_ref, ba_ref, ca_ref, xc_ref, gc_ref, wa_ref, wc_ref, bc_ref,
                   wr_ref, br_ref, wi_ref, bi_ref, lam_ref, ya_ref, yc_ref,
                   ua_buf, xc_buf, h_carry):
    ts = xa_ref.shape[0]
    ka = wa_ref.shape[0]
    kc = wc_ref.shape[0]

    @pl.when(pl.program_id(1) == 0)
    def _():
        ua_buf[0:HALO, :] = jnp.zeros((HALO, ua_buf.shape[1]), F32)
        xc_buf[0:HALO, :] = jnp.zeros((HALO, xc_buf.shape[1]), F32)
        h_carry[...] = jnp.zeros_like(h_carry)

    ua_buf[HALO:HALO + ts, :] = ca_ref[...].astype(F32) * xa_ref[...].astype(F32)
    conv_a = None
    for k in range(ka):
        term = wa_ref[k:k + 1, :] * ua_buf[HALO - (ka - 1) + k:HALO - (ka - 1) + k + ts, :]
        conv_a = term if conv_a is None else conv_a + term
    ya_ref[...] = (ba_ref[...].astype(F32) * conv_a).astype(ya_ref.dtype)
    ua_buf[0:HALO, :] = ua_buf[ts:ts + HALO, :]

    xc_buf[HALO:HALO + ts, :] = xc_ref[...].astype(F32)
    xcv = None
    for k in range(kc):
        term = wc_ref[k:k + 1, :] * xc_buf[HALO - (kc - 1) + k:HALO - (kc - 1) + k + ts, :]
        xcv = term if xcv is None else xcv + term
    xcv = xcv + bc_ref[...]
    xc_buf[0:HALO, :] = xc_buf[ts:ts + HALO, :]

    xb = xcv.astype(BF16)
    r = jax.nn.sigmoid(jnp.dot(xb, wr_ref[...], preferred_element_type=F32) + br_ref[...])
    ig = jax.nn.sigmoid(jnp.dot(xb, wi_ref[...], preferred_element_type=F32) + bi_ref[...])
    lam = lam_ref[...]
    softplus_neg_lam = jnp.maximum(-lam, 0.0) + jnp.log1p(jnp.exp(-jnp.abs(lam)))
    log_a = (-LRU_C) * r * softplus_neg_lam
    a = jnp.exp(log_a)
    u = jnp.sqrt(-jnp.tanh(log_a) * (1.0 + a * a)) * (ig * xcv)

    row = lax.broadcasted_iota(jnp.int32, a.shape, 0)
    d = 1
    while d < ts:
        a_sh = jnp.where(row >= d, pltpu.roll(a, d, 0), 1.0)
        u_sh = jnp.where(row >= d, pltpu.roll(u, d, 0), 0.0)
        u = a * u_sh + u
        a = a * a_sh
        d *= 2
    h = a * h_carry[...] + u
    h_carry[...] = h[ts - 1:ts, :]

    gc = gc_ref[...].astype(F32)
    gelu = 0.5 * gc * (1.0 + jnp.tanh(math.sqrt(2.0 / math.pi) * (gc + 0.044715 * (gc * gc * gc))))
    yc_ref[...] = (gelu * h).astype(yc_ref.dtype)


def _seqmix(p3, wa, wc, bc, wr_bd, br, wi_bd, bi, lam, ts=256):
    b, s, _ = p3.shape
    cw = wa.shape[1]
    col = lambda c: pl.BlockSpec((None, ts, cw), lambda bi_, si: (bi_, si, c))
    full = lambda arr: pl.BlockSpec(arr.shape, lambda bi_, si: (0,) * arr.ndim)
    out = jax.ShapeDtypeStruct((b, s, cw), BF16)
    ospec = pl.BlockSpec((None, ts, cw), lambda bi_, si: (bi_, si, 0))
    return pl.pallas_call(
        _seqmix_kernel,
        grid=(b, s // ts),
        in_specs=[col(0), col(1), col(2), col(6), col(7),
                  full(wa), full(wc), full(bc), full(wr_bd), full(br), full(wi_bd), full(bi), full(lam)],
        out_specs=[ospec, ospec],
        out_shape=[out, out],
        scratch_shapes=[pltpu.VMEM((HALO + ts, cw), F32), pltpu.VMEM((HALO + ts, cw), F32),
                        pltpu.VMEM((1, cw), F32)],
        compiler_params=_cparams(("parallel", "arbitrary")),
        name="seqmix",
    )(p3, p3, p3, p3, p3, wa, wc, bc, wr_bd, br, wi_bd, bi, lam)


def _dot_nt(a, b):
    return lax.dot_general(a, b, (((1,), (1,)), ((), ())), preferred_element_type=F32)


def _attn_kernel(q_ref, k_ref, v_ref, cos_ref, sa_ref, sb_ref, qg_ref, kg_ref, o_ref,
                 qa_scr, kaug, vaug, s_scr, mx_scr, m_scr, acc_scr):
    s_len = k_ref.shape[0]
    blk = MOBA_BLOCK
    nb = s_len // blk
    i = pl.program_id(2)

    @pl.when(i == 0)
    def _prep():
        li = lax.broadcasted_iota(jnp.int32, (LANES, LANES), 0) // HEAD_DIM
        lj = lax.broadcasted_iota(jnp.int32, (LANES, LANES), 1) // HEAD_DIM
        head_ones = jnp.where(li == lj, 1.0, 0.0).astype(BF16)

        def norm_rot(t_ref, gain):
            t = t_ref[...].astype(F32)
            hi, lo = _split_bf16(t * t)
            ss = (jnp.dot(hi, head_ones, preferred_element_type=F32)
                  + jnp.dot(lo, head_ones, preferred_element_type=F32))
            tn = t * lax.rsqrt(ss * (1.0 / HEAD_DIM) + EPS) * gain
            return (tn * cos_ref[...] + pltpu.roll(tn, LANES - ROT_DIM // 2, 1) * sa_ref[...]
                    + pltpu.roll(tn, ROT_DIM // 2, 1) * sb_ref[...])

        qr = norm_rot(q_ref, qg_ref[...]) * (HEAD_DIM ** -0.5 * math.log2(math.e))
        kr = norm_rot(k_ref, kg_ref[...])
        lane = lax.broadcasted_iota(jnp.int32, (s_len, LANES), 1)
        kblk = lax.broadcasted_iota(jnp.int32, (s_len, LANES), 0) // blk
        kaug[0] = jnp.where(lane < HEAD_DIM, kr,
                            jnp.where(lane - HEAD_DIM == kblk, 1.0, 0.0)).astype(BF16)
        kaug[1] = jnp.where(lane >= HEAD_DIM, kr,
                            jnp.where(lane == kblk, 1.0, 0.0)).astype(BF16)
        v = v_ref[...]
        vaug[0] = jnp.where(lane < HEAD_DIM, v, jnp.ones_like(v))
        vaug[1] = jnp.where(lane >= HEAD_DIM, v, jnp.ones_like(v))
        kmean = jnp.sum(kr.reshape(nb, blk, LANES), axis=1) * (1.0 / blk)
        ml = lax.broadcasted_iota(jnp.int32, (nb, LANES), 1)
        km0 = jnp.where(ml < HEAD_DIM, kmean, 0.0)
        km1 = jnp.where(ml >= HEAD_DIM, kmean, 0.0)
        pad = jnp.zeros((HEAD_DIM - nb, LANES), F32)
        wg = jnp.concatenate([km1, pad, km0, pad], axis=0)
        w_hi, w_lo = _split_bf16(wg)
        q_hi, q_lo = _split_bf16(qr)
        gate_t = _dot_nt(w_hi, q_hi) + _dot_nt(w_lo, q_hi) + _dot_nt(w_hi, q_lo)
        n = lax.broadcasted_iota(jnp.int32, (nb, s_len), 0)
        qblk = lax.broadcasted_iota(jnp.int32, (nb, s_len), 1) // blk

        def select_bias_t(g_rows):
            g = jnp.where(n < qblk, g_rows, -jnp.inf)
            bias = jnp.where(n == qblk, 0.0, NEG)
            for k in range(MOBA_TOPK):
                m = jnp.max(g, axis=0, keepdims=True)
                idx = jnp.min(jnp.where(g == m, n, 1 << 30), axis=0, keepdims=True)
                hit = n == idx
                bias = jnp.where(hit & (k < qblk), 0.0, bias)
                g = jnp.where(hit, -jnp.inf, g)
            return bias

        pad_t = jnp.zeros((HEAD_DIM - nb, s_len), F32)
        bias_t = jnp.concatenate([select_bias_t(gate_t[0:nb]), pad_t,
                                  select_bias_t(gate_t[HEAD_DIM:HEAD_DIM + nb]), pad_t], axis=0)
        bias = bias_t.T
        qa_scr[0] = jnp.where(lane < HEAD_DIM, qr, bias).astype(BF16)
        qa_scr[1] = jnp.where(lane >= HEAD_DIM, qr, bias).astype(BF16)

    row0 = pl.multiple_of(i * blk, blk)
    lane = lax.broadcasted_iota(jnp.int32, (blk, LANES), 1)
    mx_scr[...] = jnp.full(mx_scr.shape, -jnp.inf, F32)

    n_full = i // ATTN_CHUNK
    cw = ATTN_CHUNK * blk

    def keep_scores(c, h, s, width):
        s_scr[h, c, :, 0:width] = s
        out = s[:, 0:LANES]
        for t in range(1, width // LANES):
            out = jnp.maximum(out, s[:, t * LANES:(t + 1) * LANES])
        mx_scr[h] = jnp.maximum(mx_scr[h], out)

    def pass1(c, carry):
        r = pl.multiple_of(c * cw, cw)
        for h in range(HEADS_PER_TILE):
            keep_scores(c, h, _dot_nt(qa_scr[h, pl.ds(row0, blk), :], kaug[h, pl.ds(r, cw), :]), cw)
        return carry

    lax.fori_loop(0, n_full, pass1, 0)

    rem = i - n_full * ATTN_CHUNK
    tail0 = pl.multiple_of(n_full * cw, cw)
    for nblk in range(1, ATTN_CHUNK + 1):
        @pl.when(rem == nblk - 1)
        def _tail_scores(nblk=nblk):
            tri = (lax.broadcasted_iota(jnp.int32, (blk, blk), 1)
                   <= lax.broadcasted_iota(jnp.int32, (blk, blk), 0))
            for h in range(HEADS_PER_TILE):
                s = _dot_nt(qa_scr[h, pl.ds(row0, blk), :], kaug[h, pl.ds(tail0, nblk * blk), :])
                own = jnp.where(tri, s[:, (nblk - 1) * blk:], NEG)
                s = own if nblk == 1 else jnp.concatenate([s[:, :(nblk - 1) * blk], own], axis=1)
                keep_scores(n_full, h, s, nblk * blk)

    for h in range(HEADS_PER_TILE):
        m_scr[h] = jnp.broadcast_to(jnp.max(mx_scr[h], axis=1, keepdims=True), (blk, LANES))
    acc_scr[...] = jnp.zeros(acc_scr.shape, F32)

    def weigh(c, h, r, width):
        mb = jnp.concatenate([m_scr[h]] * (width // LANES), axis=1)
        p = jnp.exp2(s_scr[h, c, :, 0:width] - mb).astype(BF16)
        acc_scr[h] += jnp.dot(p, vaug[h, pl.ds(r, width), :], preferred_element_type=F32)

    def pass2(c, carry):
        r = pl.multiple_of(c * cw, cw)
        for h in range(HEADS_PER_TILE):
            weigh(c, h, r, cw)
        return carry

    lax.fori_loop(0, n_full, pass2, 0)
    for nblk in range(1, ATTN_CHUNK + 1):
        @pl.when(rem == nblk - 1)
        def _tail_weigh(nblk=nblk):
            for h in range(HEADS_PER_TILE):
                weigh(n_full, h, tail0, nblk * blk)

    a0 = acc_scr[0]
    a1 = acc_scr[1]
    o_ref[...] = jnp.where(lane < HEAD_DIM, a0 / pltpu.roll(a0, HEAD_DIM, 1),
                           a1 / pltpu.roll(a1, HEAD_DIM, 1)).astype(o_ref.dtype)


def _attention(p3, cos_t, sa_t, sb_t, qg, kg):
    b, s, _ = p3.shape
    n_tiles = N_HEADS // HEADS_PER_TILE
    nq = s // MOBA_BLOCK
    qcol, kcol, vcol = 12, 16, 20
    seq = lambda c0: pl.BlockSpec((None, s, LANES), lambda bi, hp, qi: (bi, 0, c0 + hp))
    tab = pl.BlockSpec((s, LANES), lambda bi, hp, qi: (0, 0))
    gain = pl.BlockSpec((1, LANES), lambda bi, hp, qi: (0, 0))
    return pl.pallas_call(
        _attn_kernel,
        grid=(b, n_tiles, nq),
        in_specs=[seq(qcol), seq(kcol), seq(vcol), tab, tab, tab, gain, gain],
        out_specs=pl.BlockSpec((None, MOBA_BLOCK, LANES), lambda bi, hp, qi: (bi, qi, hp)),
        out_shape=jax.ShapeDtypeStruct((b, s, N_HEADS * HEAD_DIM), BF16),
        scratch_shapes=[pltpu.VMEM((HEADS_PER_TILE, s, LANES), BF16),
                        pltpu.VMEM((HEADS_PER_TILE, s, LANES), BF16),
                        pltpu.VMEM((HEADS_PER_TILE, s, LANES), BF16),
                        pltpu.VMEM((HEADS_PER_TILE, nq // ATTN_CHUNK, MOBA_BLOCK, ATTN_CHUNK * MOBA_BLOCK), F32),
                        pltpu.VMEM((HEADS_PER_TILE, MOBA_BLOCK, LANES), F32),
                        pltpu.VMEM((HEADS_PER_TILE, MOBA_BLOCK, LANES), F32),
                        pltpu.VMEM((HEADS_PER_TILE, MOBA_BLOCK, LANES), F32)],
        compiler_params=_cparams(("parallel", "parallel", "arbitrary")),
        name="moba_attention",
    )(p3, p3, p3, cos_t, sa_t, sb_t, qg, kg)


def _rotary_tables(s):
    half = ROT_DIM // 2
    inv_freq = ROPE_THETA ** (-jnp.arange(0, ROT_DIM, 2, dtype=F32) / ROT_DIM)
    ang = jnp.arange(s, dtype=jnp.int32).astype(F32)[:, None] * inv_freq[None, :]
    cos, sin = jnp.cos(ang), jnp.sin(ang)
    ones = jnp.ones((s, HEAD_DIM - ROT_DIM), F32)
    zeros = jnp.zeros((s, HEAD_DIM - ROT_DIM), F32)
    zh = jnp.zeros((s, half), F32)
    cos_h = jnp.concatenate([cos, cos, ones], axis=1)
    sa_h = jnp.concatenate([-sin, zh, zeros], axis=1)
    sb_h = jnp.concatenate([zh, sin, zeros], axis=1)
    tile = lambda t: jnp.tile(t, (1, HEADS_PER_TILE))
    return tile(cos_h), tile(sa_h), tile(sb_h)


def _route(logits, n_experts):
    lane = lax.broadcasted_iota(jnp.int32, logits.shape, 1)
    g = jnp.where(lane < n_experts, logits, -jnp.inf)
    m1 = jnp.max(g, axis=1, keepdims=True)
    i1 = jnp.min(jnp.where(g == m1, lane, 1 << 30), axis=1, keepdims=True)
    g2 = jnp.where(lane == i1, -jnp.inf, g)
    m2 = jnp.max(g2, axis=1, keepdims=True)
    i2 = jnp.min(jnp.where(g2 == m2, lane, 1 << 30), axis=1, keepdims=True)
    e2 = jnp.exp(m2 - m1)
    w1 = 1.0 / (1.0 + e2)
    w2 = e2 / (1.0 + e2)
    return jnp.where(lane == 0, w1, jnp.where(lane == 1, w2, jnp.where(
        lane == 2, i1.astype(F32), jnp.where(lane == 3, i2.astype(F32), 0.0))))


def _merge_kernel(*refs, with_router, n_experts):
    if with_router:
        (x_ref, ya_ref, yb_ref, yc_ref, ga_ref, gb_ref, gcg_ref, woa_ref, wob_ref, woc_ref, wo_ref,
         gn_ref, rhi_ref, rlo_ref, x1_ref, h2_ref, route_ref) = refs
    else:
        (x_ref, ya_ref, yb_ref, yc_ref, ga_ref, gb_ref, gcg_ref, woa_ref, wob_ref, woc_ref, wo_ref,
         gn_ref, x1_ref, h2_ref) = refs
    merged = (jax.nn.sigmoid(ga_ref[...].astype(F32))
              * jnp.dot(ya_ref[...], woa_ref[...], preferred_element_type=F32)
              + jax.nn.sigmoid(gb_ref[...].astype(F32))
              * jnp.dot(yb_ref[...], wob_ref[...], preferred_element_type=F32)
              + jax.nn.sigmoid(gcg_ref[...].astype(F32))
              * jnp.dot(yc_ref[...], woc_ref[...], preferred_element_type=F32))
    x1 = x_ref[...] + jnp.dot(merged.astype(BF16), wo_ref[...], preferred_element_type=F32)
    x1_ref[...] = x1
    h2 = _rms(x1, gn_ref[...])
    if with_router:
        n_lane_tiles = h2.shape[1] // LANES
        for j in range(n_lane_tiles):
            h2_ref[pl.ds(j, h2.shape[0], stride=n_lane_tiles), :] = h2[:, j * LANES:(j + 1) * LANES]
        h_hi, h_lo = _split_bf16(h2)
        logits = (jnp.dot(h_hi, rhi_ref[...], preferred_element_type=F32)
                  + jnp.dot(h_hi, rlo_ref[...], preferred_element_type=F32)
                  + jnp.dot(h_lo, rhi_ref[...], preferred_element_type=F32))
        route_ref[...] = _route(logits, n_experts)
    else:
        h2_ref[...] = h2.astype(BF16)


def _merge(x2, ya, yb, yc, p2, woa, wob, woc, wo, gn, router=None, tm=512):
    t, d = x2.shape
    cw = ya.shape[1]
    row = lambda w: pl.BlockSpec((tm, w), lambda i: (i, 0))
    full = lambda arr: pl.BlockSpec(arr.shape, lambda i: (0,) * arr.ndim)
    gate_col0 = (p2.shape[1] - N_BRANCHES * d) // d
    gspec = lambda c: pl.BlockSpec((tm, d), lambda i: (i, gate_col0 + c))
    in_specs = [row(d), row(cw), row(cw), row(cw), gspec(0), gspec(1), gspec(2),
                full(woa), full(wob), full(woc), full(wo), full(gn)]
    args = [x2, ya, yb, yc, p2, p2, p2, woa, wob, woc, wo, gn]
    out_specs = [row(d), row(d)]
    out_shape = [jax.ShapeDtypeStruct((t, d), F32), jax.ShapeDtypeStruct((t, d), BF16)]
    n_experts = 0
    if router is not None:
        n_experts = router.shape[1]
        rpad = jnp.zeros((d, LANES), F32).at[:, :n_experts].set(router)
        r_hi, r_lo = _split_bf16(rpad)
        in_specs += [full(r_hi), full(r_lo)]
        args += [r_hi, r_lo]
        out_specs[1] = pl.BlockSpec((tm * (d // LANES), LANES), lambda i: (i, 0))
        out_shape[1] = jax.ShapeDtypeStruct((t * (d // LANES), LANES), F32)
        out_specs.append(row(LANES))
        out_shape.append(jax.ShapeDtypeStruct((t, LANES), F32))
    return pl.pallas_call(
        functools.partial(_merge_kernel, with_router=router is not None, n_experts=n_experts),
        grid=(t // tm,),
        in_specs=in_specs, out_specs=out_specs, out_shape=out_shape,
        compiler_params=_cparams(("parallel",)),
        name="merge",
    )(*args)


def _ffn_kernel(h_ref, x_ref, wg_ref, wu_ref, wd_ref, o_ref, *, tf):
    h = h_ref[...]
    out = x_ref[...]
    for c in range(wg_ref.shape[1] // tf):
        cols = slice(c * tf, (c + 1) * tf)
        g = jnp.dot(h, wg_ref[:, cols], preferred_element_type=F32)
        u = jnp.dot(h, wu_ref[:, cols], preferred_element_type=F32)
        a = (g * jax.nn.sigmoid(g) * u).astype(BF16)
        out = out + jnp.dot(a, wd_ref[cols, :], preferred_element_type=F32)
    o_ref[...] = out


def _ffn(h2, x1, wg, wu, wd, tm=512, tf=1408):
    t, d = x1.shape
    ff = wg.shape[1]
    resident = lambda arr: pl.BlockSpec(arr.shape, lambda i: (0, 0), pipeline_mode=pl.Buffered(1))
    return pl.pallas_call(
        functools.partial(_ffn_kernel, tf=tf),
        grid=(t // tm,),
        in_specs=[pl.BlockSpec((tm, d), lambda i: (i, 0)),
                  pl.BlockSpec((tm, d), lambda i: (i, 0)),
                  resident(wg), resident(wu), resident(wd)],
        out_specs=pl.BlockSpec((tm, d), lambda i: (i, 0)),
        out_shape=jax.ShapeDtypeStruct((t, d), F32),
        compiler_params=_cparams(("parallel",)),
        name="dense_ffn",
    )(h2, x1, wg, wu, wd)


def _moe_plan(route, n_experts, tm):
    t = route.shape[0]
    ids = route[:, 2:2 + TOP_K].astype(jnp.int32).reshape(-1)
    onehot = (ids[:, None] == jnp.arange(n_experts, dtype=jnp.int32)[None, :]).astype(jnp.int32)
    csum = jnp.cumsum(onehot, axis=0)
    rank = jnp.sum((csum - onehot) * onehot, axis=1)
    counts = csum[-1]
    gsize = (counts + tm - 1) // tm * tm
    gend = jnp.cumsum(gsize)
    gstart = gend - gsize
    dest = gstart[ids] + rank
    n_tiles = (TOP_K * t) // tm + n_experts
    n_rows = n_tiles * tm
    pair = jnp.full((n_rows,), -1, jnp.int32).at[dest].set(jnp.arange(TOP_K * t, dtype=jnp.int32))
    row = jnp.arange(n_rows, dtype=jnp.int32)
    tok = jnp.where(pair >= 0, pair // TOP_K, 0)
    ydst = jnp.where(pair >= 0, (pair % TOP_K) * t + pair // TOP_K, TOP_K * t + row % tm)
    tile_start = jnp.arange(n_tiles, dtype=jnp.int32) * tm
    tile_valid = (tile_start < gend[-1]).astype(jnp.int32)
    tile_expert = jnp.minimum(jnp.sum((tile_start[:, None] >= gend[None, :]).astype(jnp.int32), axis=1),
                              n_experts - 1)
    last_expert = tile_expert[jnp.maximum(jnp.sum(tile_valid) - 1, 0)]
    tile_expert = jnp.where(tile_valid == 1, tile_expert, last_expert)
    spare = TOP_K * t + jnp.arange(tm, dtype=jnp.int32)
    ydst = jnp.concatenate([spare[None, :], ydst.reshape(n_tiles, tm)], axis=0)
    return tok.reshape(n_tiles, tm), ydst, tile_expert, tile_valid


def _moe_ffn_kernel(te_ref, tv_ref, tok_hbm, ydst_hbm, h_hbm, wg_ref, wu_ref, wd_ref, y_hbm,
                    x_buf, x_bf, acc, y_buf, tok_smem, ydst_smem, sem_idx, sem_gather, sem_scatter,
                    *, n_tiles, nf):
    r = pl.program_id(0)
    f = pl.program_id(1)
    tm = acc.shape[0]
    n_lane_tiles = acc.shape[1] // LANES
    share = tm // nf
    valid = tv_ref[r] == 1

    def idx_copies(tile):
        src_tile = jnp.minimum(tile, n_tiles - 1)
        dst = pl.ds(pl.multiple_of((tile & 3) * tm, tm), tm)
        return (pltpu.make_async_copy(tok_hbm.at[src_tile], tok_smem.at[dst], sem_idx.at[0]),
                pltpu.make_async_copy(ydst_hbm.at[src_tile], ydst_smem.at[dst], sem_idx.at[1]))

    def row_tile(buf, k):
        return buf.at[pl.ds(pl.multiple_of(k * n_lane_tiles, n_lane_tiles), n_lane_tiles), :]

    def gather_wait():
        pltpu.make_async_copy(x_buf, x_buf, sem_gather).wait()

    def scatter_wait():
        pltpu.make_async_copy(y_buf, y_buf, sem_scatter).wait()

    def start_gather(slot, k):
        pltpu.make_async_copy(h_hbm.at[tok_smem[slot * tm + k]], row_tile(x_buf, k), sem_gather).start()

    def start_scatter(slot, k):
        pltpu.make_async_copy(row_tile(y_buf, k), y_hbm.at[ydst_smem[slot * tm + k]], sem_scatter).start()

    def issue_share():
        nxt = (r + 1) & 3
        cur = r & 3
        for j in range(share):
            k = f * share + j
            start_gather(nxt, k)
            start_scatter(cur, k)

    @pl.when((r == 0) & (f == 0))
    def _first():
        acc[...] = jnp.zeros(acc.shape, F32)
        for c in idx_copies(0):
            c.start()
        for c in idx_copies(0):
            c.wait()

        def issue(k, carry):
            start_gather(0, k)
            return carry

        lax.fori_loop(0, tm, issue, 0, unroll=8)
        for c in idx_copies(1):
            c.start()

    @pl.when(f == 0)
    def _tile_start():
        @pl.when(r >= 1)
        def _():
            scatter_wait()

        for j in range(n_lane_tiles):
            y_buf[pl.ds(j, tm, stride=n_lane_tiles), :] = acc[:, j * LANES:(j + 1) * LANES]
        gather_wait()
        for j in range(n_lane_tiles):
            x_bf[:, j * LANES:(j + 1) * LANES] = x_buf[pl.ds(j, tm, stride=n_lane_tiles), :].astype(BF16)
        for c in idx_copies(r + 1):
            c.wait()
        for c in idx_copies(r + 2):
            c.start()

    @pl.when(valid)
    def _compute():
        issue_share()
        x = x_bf[...]
        g = jnp.dot(x, wg_ref[...], preferred_element_type=F32)
        u = jnp.dot(x, wu_ref[...], preferred_element_type=F32)
        a = (g * jax.nn.sigmoid(g) * u).astype(BF16)
        part = jnp.dot(a, wd_ref[...], preferred_element_type=F32)

        @pl.when(f == 0)
        def _():
            acc[...] = part

        @pl.when(f > 0)
        def _():
            acc[...] += part

    @pl.when(jnp.logical_not(valid))
    def _idle():
        issue_share()

    @pl.when((r == n_tiles - 1) & (f == nf - 1))
    def _drain():
        gather_wait()
        scatter_wait()
        for c in idx_copies(r + 2):
            c.wait()


def _moe_ffn(h2, tok, ydst, tile_expert, tile_valid, wg, wu, wd, tf=896):
    d = wg.shape[1]
    n_lane_tiles = d // LANES
    t = h2.shape[0] // n_lane_tiles
    h2 = h2.reshape(t, n_lane_tiles, LANES)
    n_tiles, tm = tok.shape
    ff = wg.shape[2]
    nf = ff // tf
    assert tm % nf == 0 and ydst.shape == (n_tiles + 1, tm)
    fsel = lambda f, tv, r: jnp.where(tv[r] == 1, f, nf - 1)
    grid_spec = pltpu.PrefetchScalarGridSpec(
        num_scalar_prefetch=2,
        grid=(n_tiles, nf),
        in_specs=[pl.BlockSpec(memory_space=pl.ANY), pl.BlockSpec(memory_space=pl.ANY),
                  pl.BlockSpec(memory_space=pl.ANY),
                  pl.BlockSpec((None, d, tf), lambda r, f, te, tv: (te[r], 0, fsel(f, tv, r))),
                  pl.BlockSpec((None, d, tf), lambda r, f, te, tv: (te[r], 0, fsel(f, tv, r))),
                  pl.BlockSpec((None, tf, d), lambda r, f, te, tv: (te[r], fsel(f, tv, r), 0))],
        out_specs=pl.BlockSpec(memory_space=pl.ANY),
        scratch_shapes=[pltpu.VMEM((tm * n_lane_tiles, LANES), F32), pltpu.VMEM((tm, d), BF16),
                        pltpu.VMEM((tm, d), F32), pltpu.VMEM((tm * n_lane_tiles, LANES), F32),
                        pltpu.SMEM((4 * tm,), jnp.int32), pltpu.SMEM((4 * tm,), jnp.int32),
                        pltpu.SemaphoreType.DMA((2,)), pltpu.SemaphoreType.DMA, pltpu.SemaphoreType.DMA],
    )
    return pl.pallas_call(
        functools.partial(_moe_ffn_kernel, n_tiles=n_tiles, nf=nf),
        grid_spec=grid_spec,
        out_shape=jax.ShapeDtypeStruct((TOP_K * t + tm, n_lane_tiles, LANES), F32),
        compiler_params=_cparams(("arbitrary", "arbitrary")),
        name="moe_ffn",
    )(tile_expert, tile_valid, tok, ydst, h2, wg, wu, wd)


def _combine_kernel(x_ref, y0_ref, y1_ref, route_ref, o_ref):
    route = route_ref[...]
    lane = lax.broadcasted_iota(jnp.int32, route.shape, 1)
    w0 = jnp.sum(jnp.where(lane == 0, route, 0.0), axis=1, keepdims=True)
    w1 = jnp.sum(jnp.where(lane == 1, route, 0.0), axis=1, keepdims=True)
    tm, d = x_ref.shape
    n_lane_tiles = d // LANES
    for j in range(n_lane_tiles):
        cols = slice(j * LANES, (j + 1) * LANES)
        rows = pl.ds(j, tm, stride=n_lane_tiles)
        o_ref[:, cols] = x_ref[:, cols] + w0 * y0_ref[rows, :] + w1 * y1_ref[rows, :]


def _combine(x1, y, route, tm=512):
    t, d = x1.shape
    nblk = t // tm
    return pl.pallas_call(
        _combine_kernel,
        grid=(nblk,),
        in_specs=[pl.BlockSpec((tm, d), lambda i: (i, 0)),
                  pl.BlockSpec((tm * (d // LANES), LANES), lambda i: (i, 0)),
                  pl.BlockSpec((tm * (d // LANES), LANES), lambda i: (i + nblk, 0)),
                  pl.BlockSpec((tm, LANES), lambda i: (i, 0))],
        out_specs=pl.BlockSpec((tm, d), lambda i: (i, 0)),
        out_shape=jax.ShapeDtypeStruct((t, d), F32),
        compiler_params=_cparams(("parallel",)),
        name="moe_combine",
    )(x1, y, y, route)


def _moe(h2, x1, route, wg, wu, wd, tm=1024):
    tok, ydst, tile_expert, tile_valid = _moe_plan(route, wg.shape[0], tm)
    y = _moe_ffn(h2, tok, ydst, tile_expert, tile_valid, wg, wu, wd)
    return _combine(x1, y.reshape(-1, LANES), route)


def _block_diag(w):
    nb, n, _ = w.shape
    eye = jnp.eye(nb, dtype=w.dtype)
    return (eye[:, None, :, None] * w[:, :, None, :]).reshape(nb * n, nb * n)


def kernel(x, norm_mix, w_in, conv_a_w, wo_a, q_norm, k_norm, wo_b, conv_c_w, conv_c_b, rg_w_r, rg_b_r, rg_w_i, rg_b_i, rg_lambda, wo_c, w_o, norm_ffn, ffn_w_gate, ffn_w_up, ffn_w_down, moe_router, moe_w_gate, moe_w_up, moe_w_down):
    bsz, s, d = x.shape
    depth = w_in.shape[0]
    cos_t, sa_t, sb_t = _rotary_tables(s)
    x2 = x.reshape(bsz * s, d)
    for l in range(depth):
        p2 = _inproj(x2, norm_mix[l][None, :], w_in[l].astype(BF16))
        p3 = p2.reshape(bsz, s, p2.shape[1])
        ya, yc = _seqmix(p3, conv_a_w[l], conv_c_w[l], conv_c_b[l][None, :],
                         _block_diag(rg_w_r[l]).astype(BF16), rg_b_r[l][None, :],
                         _block_diag(rg_w_i[l]).astype(BF16), rg_b_i[l][None, :],
                         rg_lambda[l][None, :])
        yb = _attention(p3, cos_t, sa_t, sb_t,
                        jnp.tile(q_norm[l], HEADS_PER_TILE)[None, :],
                        jnp.tile(k_norm[l], HEADS_PER_TILE)[None, :])
        cw = ya.shape[2]
        args = (x2, ya.reshape(bsz * s, cw), yb.reshape(bsz * s, yb.shape[2]), yc.reshape(bsz * s, cw), p2,
                wo_a[l].astype(BF16), wo_b[l].astype(BF16), wo_c[l].astype(BF16), w_o[l].astype(BF16),
                norm_ffn[l][None, :])
        if l % 2 == 0:
            x1, h2 = _merge(*args)
            x2 = _ffn(h2, x1, ffn_w_gate[l // 2].astype(BF16), ffn_w_up[l // 2].astype(BF16),
                      ffn_w_down[l // 2].astype(BF16))
        else:
            x1, h2, route = _merge(*args, router=moe_router[l // 2])
            x2 = _moe(h2, x1, route, moe_w_gate[l // 2].astype(BF16), moe_w_up[l // 2].astype(BF16),
                      moe_w_down[l // 2].astype(BF16))
    return x2.reshape(bsz, s, d)
```

```python
import functools
import math

import jax
import jax.numpy as jnp
from jax import lax
from jax.experimental import pallas as pl
from jax.experimental.pallas import tpu as pltpu

F32 = jnp.float32
BF16 = jnp.bfloat16

N_HEADS = 8
HEAD_DIM = 64
ROT_DIM = HEAD_DIM // 4
ROPE_THETA = 500000.0
MOBA_BLOCK = 256
MOBA_TOPK = 3
LRU_C = 8.0
N_BRANCHES = 3
TOP_K = 2
EPS = 1e-6
NEG = -1e30

LANES = 128
SUBLANES = 8
HEADS_PER_TILE = LANES // HEAD_DIM
ATTN_CHUNK = 8
VMEM_LIMIT = 56 * 1024 * 1024


def _cparams(sem):
    return pltpu.CompilerParams(dimension_semantics=sem, vmem_limit_bytes=VMEM_LIMIT)


def _split_bf16(x):
    hi = x.astype(BF16)
    lo = (x - hi.astype(F32)).astype(BF16)
    return hi, lo


def _rms(x, g):
    ms = jnp.mean(x * x, axis=-1, keepdims=True)
    return x * lax.rsqrt(ms + EPS) * g


def _inproj_kernel(x_ref, g_ref, w_ref, o_ref, *, tn):
    h = _rms(x_ref[...], g_ref[...]).astype(BF16)
    for c in range(o_ref.shape[1] // tn):
        cols = slice(c * tn, (c + 1) * tn)
        o_ref[:, cols] = jnp.dot(h, w_ref[:, cols], preferred_element_type=F32).astype(o_ref.dtype)


def _inproj(x2, g, w, tm=512, tn=1024):
    t, d = x2.shape
    n = w.shape[1]
    return pl.pallas_call(
        functools.partial(_inproj_kernel, tn=tn),
        grid=(t // tm,),
        in_specs=[pl.BlockSpec((tm, d), lambda i: (i, 0)),
                  pl.BlockSpec((1, d), lambda i: (0, 0)),
                  pl.BlockSpec((d, n), lambda i: (0, 0), pipeline_mode=pl.Buffered(1))],
        out_specs=pl.BlockSpec((tm, n), lambda i: (i, 0)),
        out_shape=jax.ShapeDtypeStruct((t, n), BF16),
        compiler_params=_cparams(("parallel",)),
        name="inproj",
    )(x2, g, w)


HALO = 8


def _seqmix_kernel(xa_ref, ba_ref, ca_ref, xc_ref, gc_ref, wa_ref, wc_ref, bc_ref,
                   wr_ref, br_ref, wi_ref, bi_ref, lam_ref, ya_ref, yc_ref,
                   ua_buf, xc_buf, h_carry):
    ts = xa_ref.shape[0]

    @pl.when(pl.program_id(1) == 0)
    def _():
        ua_buf[...] = jnp.zeros(ua_buf.shape, F32)
        xc_buf[...] = jnp.zeros(xc_buf.shape, F32)
        h_carry[...] = jnp.zeros_like(h_carry)

    def causal_conv(u, w_ref, tail_ref):
        k_w = w_ref.shape[0]
        row = lax.broadcasted_iota(jnp.int32, u.shape, 0)
        prev = jnp.concatenate([tail_ref[...]] * (ts // HALO), axis=0)
        y = w_ref[k_w - 1:k_w, :] * u
        for d in range(1, k_w):
            shifted = jnp.where(row >= d, pltpu.roll(u, d, 0), pltpu.roll(prev, d, 0))
            y = y + w_ref[k_w - 1 - d:k_w - d, :] * shifted
        tail_ref[...] = u[ts - HALO:ts, :]
        return y

    conv_a = causal_conv(ca_ref[...].astype(F32) * xa_ref[...].astype(F32), wa_ref, ua_buf)
    ya_ref[...] = (ba_ref[...].astype(F32) * conv_a).astype(ya_ref.dtype)

    xcv = causal_conv(xc_ref[...].astype(F32), wc_ref, xc_buf) + bc_ref[...]


    xb = xcv.astype(BF16)
    r = jax.nn.sigmoid(jnp.dot(xb, wr_ref[...], preferred_element_type=F32) + br_ref[...])
    ig = jax.nn.sigmoid(jnp.dot(xb, wi_ref[...], preferred_element_type=F32) + bi_ref[...])
    lam = lam_ref[...]
    softplus_neg_lam = jnp.maximum(-lam, 0.0) + jnp.log1p(jnp.exp(-jnp.abs(lam)))
    log_a = (-LRU_C) * r * softplus_neg_lam
    a = jnp.exp(log_a)
    u = jnp.sqrt(-jnp.tanh(log_a) * (1.0 + a * a)) * (ig * xcv)

    n_groups = ts // SUBLANES
    a = a.reshape(n_groups, SUBLANES, a.shape[1])
    u = u.reshape(n_groups, SUBLANES, u.shape[1])
    sub = lax.broadcasted_iota(jnp.int32, a.shape, 1)
    d = 1
    while d < SUBLANES:
        a_sh = jnp.where(sub >= d, pltpu.roll(a, d, 1), 1.0)
        u_sh = jnp.where(sub >= d, pltpu.roll(u, d, 1), 0.0)
        u = a * u_sh + u
        a = a * a_sh
        d *= 2
    carry = h_carry[...]
    groups = []
    for g in range(n_groups):
        hg = a[g] * carry + u[g]
        carry = hg[SUBLANES - 1:SUBLANES, :]
        groups.append(hg)
    h = jnp.concatenate(groups, axis=0)
    h_carry[...] = carry

    gc = gc_ref[...].astype(F32)
    gelu = 0.5 * gc * (1.0 + jnp.tanh(math.sqrt(2.0 / math.pi) * (gc + 0.044715 * (gc * gc * gc))))
    yc_ref[...] = (gelu * h).astype(yc_ref.dtype)


def _seqmix(p3, wa, wc, bc, wr_bd, br, wi_bd, bi, lam, ts=256):
    b, s, _ = p3.shape
    cw = wa.shape[1]
    col = lambda c: pl.BlockSpec((None, ts, cw), lambda bi_, si: (bi_, si, c))
    full = lambda arr: pl.BlockSpec(arr.shape, lambda bi_, si: (0,) * arr.ndim)
    out = jax.ShapeDtypeStruct((b, s, cw), BF16)
    ospec = pl.BlockSpec((None, ts, cw), lambda bi_, si: (bi_, si, 0))
    return pl.pallas_call(
        _seqmix_kernel,
        grid=(b, s // ts),
        in_specs=[col(0), col(1), col(2), col(6), col(7),
                  full(wa), full(wc), full(bc), full(wr_bd), full(br), full(wi_bd), full(bi), full(lam)],
        out_specs=[ospec, ospec],
        out_shape=[out, out],
        scratch_shapes=[pltpu.VMEM((HALO, cw), F32), pltpu.VMEM((HALO, cw), F32),
                        pltpu.VMEM((1, cw), F32)],
        compiler_params=_cparams(("parallel", "arbitrary")),
        name="seqmix",
    )(p3, p3, p3, p3, p3, wa, wc, bc, wr_bd, br, wi_bd, bi, lam)


def _dot_nt(a, b):
    return lax.dot_general(a, b, (((1,), (1,)), ((), ())), preferred_element_type=F32)


def _attn_kernel(q_ref, k_ref, v_ref, cos_ref, sa_ref, sb_ref, qg_ref, kg_ref, o_ref,
                 qa_scr, kaug, vaug, s_scr, mx_scr, m_scr, acc_scr):
    s_len = k_ref.shape[0]
    blk = MOBA_BLOCK
    nb = s_len // blk
    i = pl.program_id(2)

    @pl.when(i == 0)
    def _prep():
        li = lax.broadcasted_iota(jnp.int32, (LANES, LANES), 0) // HEAD_DIM
        lj = lax.broadcasted_iota(jnp.int32, (LANES, LANES), 1) // HEAD_DIM
        head_ones = jnp.where(li == lj, 1.0, 0.0).astype(BF16)

        def norm_rot(t_ref, gain):
            t = t_ref[...].astype(F32)
            hi, lo = _split_bf16(t * t)
            ss = (jnp.dot(hi, head_ones, preferred_element_type=F32)
                  + jnp.dot(lo, head_ones, preferred_element_type=F32))
            tn = t * lax.rsqrt(ss * (1.0 / HEAD_DIM) + EPS) * gain
            return (tn * cos_ref[...] + pltpu.roll(tn, LANES - ROT_DIM // 2, 1) * sa_ref[...]
                    + pltpu.roll(tn, ROT_DIM // 2, 1) * sb_ref[...])

        qr = norm_rot(q_ref, qg_ref[...]) * (HEAD_DIM ** -0.5 * math.log2(math.e))
        kr = norm_rot(k_ref, kg_ref[...])
        lane = lax.broadcasted_iota(jnp.int32, (s_len, LANES), 1)
        kblk = lax.broadcasted_iota(jnp.int32, (s_len, LANES), 0) // blk
        kaug[0] = jnp.where(lane < HEAD_DIM, kr,
                            jnp.where(lane - HEAD_DIM == kblk, 1.0, 0.0)).astype(BF16)
        kaug[1] = jnp.where(lane >= HEAD_DIM, kr,
                            jnp.where(lane == kblk, 1.0, 0.0)).astype(BF16)
        v = v_ref[...]
        vaug[0] = jnp.where(lane < HEAD_DIM, v, jnp.ones_like(v))
        vaug[1] = jnp.where(lane >= HEAD_DIM, v, jnp.ones_like(v))
        kmean = jnp.sum(kr.reshape(nb, blk, LANES), axis=1) * (1.0 / blk)
        ml = lax.broadcasted_iota(jnp.int32, (nb, LANES), 1)
        km0 = jnp.where(ml < HEAD_DIM, kmean, 0.0)
        km1 = jnp.where(ml >= HEAD_DIM, kmean, 0.0)
        pad = jnp.zeros((HEAD_DIM - nb, LANES), F32)
        wg = jnp.concatenate([km1, pad, km0, pad], axis=0)
        w_hi, w_lo = _split_bf16(wg)
        q_hi, q_lo = _split_bf16(qr)
        gate_t = _dot_nt(w_hi, q_hi) + _dot_nt(w_lo, q_hi) + _dot_nt(w_hi, q_lo)
        n = lax.broadcasted_iota(jnp.int32, (nb, s_len), 0)
        qblk = lax.broadcasted_iota(jnp.int32, (nb, s_len), 1) // blk

        def select_bias_t(g_rows):
            g = jnp.where(n < qblk, g_rows, -jnp.inf)
            bias = jnp.where(n == qblk, 0.0, NEG)
            for k in range(MOBA_TOPK):
                m = jnp.max(g, axis=0, keepdims=True)
                idx = jnp.min(jnp.where(g == m, n, 1 << 30), axis=0, keepdims=True)
                hit = n == idx
                bias = jnp.where(hit & (k < qblk), 0.0, bias)
                g = jnp.where(hit, -jnp.inf, g)
            return bias

        pad_t = jnp.zeros((HEAD_DIM - nb, s_len), F32)
        bias_t = jnp.concatenate([select_bias_t(gate_t[0:nb]), pad_t,
                                  select_bias_t(gate_t[HEAD_DIM:HEAD_DIM + nb]), pad_t], axis=0)
        bias = bias_t.T
        qa_scr[0] = jnp.where(lane < HEAD_DIM, qr, bias).astype(BF16)
        qa_scr[1] = jnp.where(lane >= HEAD_DIM, qr, bias).astype(BF16)

    row0 = pl.multiple_of(i * blk, blk)
    lane = lax.broadcasted_iota(jnp.int32, (blk, LANES), 1)
    mx_scr[...] = jnp.full(mx_scr.shape, -jnp.inf, F32)

    n_full = i // ATTN_CHUNK
    cw = ATTN_CHUNK * blk

    def keep_scores(c, h, s, width):
        s_scr[h, c, :, 0:width] = s
        out = s[:, 0:LANES]
        for t in range(1, width // LANES):
            out = jnp.maximum(out, s[:, t * LANES:(t + 1) * LANES])
        mx_scr[h] = jnp.maximum(mx_scr[h], out)

    def pass1(c, carry):
        r = pl.multiple_of(c * cw, cw)
        for h in range(HEADS_PER_TILE):
            keep_scores(c, h, _dot_nt(qa_scr[h, pl.ds(row0, blk), :], kaug[h, pl.ds(r, cw), :]), cw)
        return carry

    lax.fori_loop(0, n_full, pass1, 0)

    rem = i - n_full * ATTN_CHUNK
    tail0 = pl.multiple_of(n_full * cw, cw)
    for nblk in range(1, ATTN_CHUNK + 1):
        @pl.when(rem == nblk - 1)
        def _tail_scores(nblk=nblk):
            tri = (lax.broadcasted_iota(jnp.int32, (blk, blk), 1)
                   <= lax.broadcasted_iota(jnp.int32, (blk, blk), 0))
            for h in range(HEADS_PER_TILE):
                s = _dot_nt(qa_scr[h, pl.ds(row0, blk), :], kaug[h, pl.ds(tail0, nblk * blk), :])
                own = jnp.where(tri, s[:, (nblk - 1) * blk:], NEG)
                s = own if nblk == 1 else jnp.concatenate([s[:, :(nblk - 1) * blk], own], axis=1)
                keep_scores(n_full, h, s, nblk * blk)

    for h in range(HEADS_PER_TILE):
        m_scr[h] = jnp.broadcast_to(jnp.max(mx_scr[h], axis=1, keepdims=True), (blk, LANES))
    acc_scr[...] = jnp.zeros(acc_scr.shape, F32)

    def weigh(c, h, r, width):
        mb = jnp.concatenate([m_scr[h]] * (width // LANES), axis=1)
        p = jnp.exp2(s_scr[h, c, :, 0:width] - mb).astype(BF16)
        acc_scr[h] += jnp.dot(p, vaug[h, pl.ds(r, width), :], preferred_element_type=F32)

    def pass2(c, carry):
        r = pl.multiple_of(c * cw, cw)
        for h in range(HEADS_PER_TILE):
            weigh(c, h, r, cw)
        return carry

    lax.fori_loop(0, n_full, pass2, 0)
    for nblk in range(1, ATTN_CHUNK + 1):
        @pl.when(rem == nblk - 1)
        def _tail_weigh(nblk=nblk):
            for h in range(HEADS_PER_TILE):
                weigh(n_full, h, tail0, nblk * blk)

    a0 = acc_scr[0]
    a1 = acc_scr[1]
    o_ref[...] = jnp.where(lane < HEAD_DIM, a0 / pltpu.roll(a0, HEAD_DIM, 1),
                           a1 / pltpu.roll(a1, HEAD_DIM, 1)).astype(o_ref.dtype)


def _attention(p3, cos_t, sa_t, sb_t, qg, kg):
    b, s, _ = p3.shape
    n_tiles = N_HEADS // HEADS_PER_TILE
    nq = s // MOBA_BLOCK
    qcol, kcol, vcol = 12, 16, 20
    seq = lambda c0: pl.BlockSpec((None, s, LANES), lambda bi, hp, qi: (bi, 0, c0 + hp))
    tab = pl.BlockSpec((s, LANES), lambda bi, hp, qi: (0, 0))
    gain = pl.BlockSpec((1, LANES), lambda bi, hp, qi: (0, 0))
    return pl.pallas_call(
        _attn_kernel,
        grid=(b, n_tiles, nq),
        in_specs=[seq(qcol), seq(kcol), seq(vcol), tab, tab, tab, gain, gain],
        out_specs=pl.BlockSpec((None, MOBA_BLOCK, LANES), lambda bi, hp, qi: (bi, qi, hp)),
        out_shape=jax.ShapeDtypeStruct((b, s, N_HEADS * HEAD_DIM), BF16),
        scratch_shapes=[pltpu.VMEM((HEADS_PER_TILE, s, LANES), BF16),
                        pltpu.VMEM((HEADS_PER_TILE, s, LANES), BF16),
                        pltpu.VMEM((HEADS_PER_TILE, s, LANES), BF16),
                        pltpu.VMEM((HEADS_PER_TILE, nq // ATTN_CHUNK, MOBA_BLOCK, ATTN_CHUNK * MOBA_BLOCK), F32),
                        pltpu.VMEM((HEADS_PER_TILE, MOBA_BLOCK, LANES), F32),
                        pltpu.VMEM((HEADS_PER_TILE, MOBA_BLOCK, LANES), F32),
                        pltpu.VMEM((HEADS_PER_TILE, MOBA_BLOCK, LANES), F32)],
        compiler_params=_cparams(("parallel", "parallel", "arbitrary")),
        name="moba_attention",
    )(p3, p3, p3, cos_t, sa_t, sb_t, qg, kg)


def _rotary_tables(s):
    half = ROT_DIM // 2
    inv_freq = ROPE_THETA ** (-jnp.arange(0, ROT_DIM, 2, dtype=F32) / ROT_DIM)
    ang = jnp.arange(s, dtype=jnp.int32).astype(F32)[:, None] * inv_freq[None, :]
    cos, sin = jnp.cos(ang), jnp.sin(ang)
    ones = jnp.ones((s, HEAD_DIM - ROT_DIM), F32)
    zeros = jnp.zeros((s, HEAD_DIM - ROT_DIM), F32)
    zh = jnp.zeros((s, half), F32)
    cos_h = jnp.concatenate([cos, cos, ones], axis=1)
    sa_h = jnp.concatenate([-sin, zh, zeros], axis=1)
    sb_h = jnp.concatenate([zh, sin, zeros], axis=1)
    tile = lambda t: jnp.tile(t, (1, HEADS_PER_TILE))
    return tile(cos_h), tile(sa_h), tile(sb_h)


def _route(logits, n_experts):
    lane = lax.broadcasted_iota(jnp.int32, logits.shape, 1)
    g = jnp.where(lane < n_experts, logits, -jnp.inf)
    m1 = jnp.max(g, axis=1, keepdims=True)
    i1 = jnp.min(jnp.where(g == m1, lane, 1 << 30), axis=1, keepdims=True)
    g2 = jnp.where(lane == i1, -jnp.inf, g)
    m2 = jnp.max(g2, axis=1, keepdims=True)
    i2 = jnp.min(jnp.where(g2 == m2, lane, 1 << 30), axis=1, keepdims=True)
    e2 = jnp.exp(m2 - m1)
    w1 = 1.0 / (1.0 + e2)
    w2 = e2 / (1.0 + e2)
    return jnp.where(lane == 0, w1, jnp.where(lane == 1, w2, jnp.where(
        lane == 2, i1.astype(F32), jnp.where(lane == 3, i2.astype(F32), 0.0))))


def _merge_kernel(*refs, with_router, n_experts):
    if with_router:
        (x_ref, ya_ref, yb_ref, yc_ref, ga_ref, gb_ref, gcg_ref, woa_ref, wob_ref, woc_ref, wo_ref,
         gn_ref, rhi_ref, rlo_ref, x1_ref, h2_ref, route_ref) = refs
    else:
        (x_ref, ya_ref, yb_ref, yc_ref, ga_ref, gb_ref, gcg_ref, woa_ref, wob_ref, woc_ref, wo_ref,
         gn_ref, x1_ref, h2_ref) = refs
    merged = (jax.nn.sigmoid(ga_ref[...].astype(F32))
              * jnp.dot(ya_ref[...], woa_ref[...], preferred_element_type=F32)
              + jax.nn.sigmoid(gb_ref[...].astype(F32))
              * jnp.dot(yb_ref[...], wob_ref[...], preferred_element_type=F32)
              + jax.nn.sigmoid(gcg_ref[...].astype(F32))
              * jnp.dot(yc_ref[...], woc_ref[...], preferred_element_type=F32))
    x1 = x_ref[...] + jnp.dot(merged.astype(BF16), wo_ref[...], preferred_element_type=F32)
    x1_ref[...] = x1
    h2 = _rms(x1, gn_ref[...])
    if with_router:
        n_lane_tiles = h2.shape[1] // LANES
        for j in range(n_lane_tiles):
            h2_ref[pl.ds(j, h2.shape[0], stride=n_lane_tiles), :] = h2[:, j * LANES:(j + 1) * LANES]
        h_hi, h_lo = _split_bf16(h2)
        logits = (jnp.dot(h_hi, rhi_ref[...], preferred_element_type=F32)
                  + jnp.dot(h_hi, rlo_ref[...], preferred_element_type=F32)
                  + jnp.dot(h_lo, rhi_ref[...], preferred_element_type=F32))
        route_ref[...] = _route(logits, n_experts)
    else:
        h2_ref[...] = h2.astype(BF16)


def _merge(x2, ya, yb, yc, p2, woa, wob, woc, wo, gn, router=None, tm=512):
    t, d = x2.shape
    cw = ya.shape[1]
    row = lambda w: pl.BlockSpec((tm, w), lambda i: (i, 0))
    full = lambda arr: pl.BlockSpec(arr.shape, lambda i: (0,) * arr.ndim)
    gate_col0 = (p2.shape[1] - N_BRANCHES * d) // d
    gspec = lambda c: pl.BlockSpec((tm, d), lambda i: (i, gate_col0 + c))
    in_specs = [row(d), row(cw), row(cw), row(cw), gspec(0), gspec(1), gspec(2),
                full(woa), full(wob), full(woc), full(wo), full(gn)]
    args = [x2, ya, yb, yc, p2, p2, p2, woa, wob, woc, wo, gn]
    out_specs = [row(d), row(d)]
    out_shape = [jax.ShapeDtypeStruct((t, d), F32), jax.ShapeDtypeStruct((t, d), BF16)]
    n_experts = 0
    if router is not None:
        n_experts = router.shape[1]
        rpad = jnp.zeros((d, LANES), F32).at[:, :n_experts].set(router)
        r_hi, r_lo = _split_bf16(rpad)
        in_specs += [full(r_hi), full(r_lo)]
        args += [r_hi, r_lo]
        out_specs[1] = pl.BlockSpec((tm * (d // LANES), LANES), lambda i: (i, 0))
        out_shape[1] = jax.ShapeDtypeStruct((t * (d // LANES), LANES), F32)
        out_specs.append(row(LANES))
        out_shape.append(jax.ShapeDtypeStruct((t, LANES), F32))
    return pl.pallas_call(
        functools.partial(_merge_kernel, with_router=router is not None, n_experts=n_experts),
        grid=(t // tm,),
        in_specs=in_specs, out_specs=out_specs, out_shape=out_shape,
        compiler_params=_cparams(("parallel",)),
        name="merge",
    )(*args)


def _ffn_kernel(h_ref, x_ref, wg_ref, wu_ref, wd_ref, o_ref, *, tf):
    h = h_ref[...]
    out = x_ref[...]
    for c in range(wg_ref.shape[1] // tf):
        cols = slice(c * tf, (c + 1) * tf)
        g = jnp.dot(h, wg_ref[:, cols], preferred_element_type=F32)
        u = jnp.dot(h, wu_ref[:, cols], preferred_element_type=F32)
        a = (g * jax.nn.sigmoid(g) * u).astype(BF16)
        out = out + jnp.dot(a, wd_ref[cols, :], preferred_element_type=F32)
    o_ref[...] = out


def _ffn(h2, x1, wg, wu, wd, tm=512, tf=1408):
    t, d = x1.shape
    ff = wg.shape[1]
    resident = lambda arr: pl.BlockSpec(arr.shape, lambda i: (0, 0), pipeline_mode=pl.Buffered(1))
    return pl.pallas_call(
        functools.partial(_ffn_kernel, tf=tf),
        grid=(t // tm,),
        in_specs=[pl.BlockSpec((tm, d), lambda i: (i, 0)),
                  pl.BlockSpec((tm, d), lambda i: (i, 0)),
                  resident(wg), resident(wu), resident(wd)],
        out_specs=pl.BlockSpec((tm, d), lambda i: (i, 0)),
        out_shape=jax.ShapeDtypeStruct((t, d), F32),
        compiler_params=_cparams(("parallel",)),
        name="dense_ffn",
    )(h2, x1, wg, wu, wd)


def _moe_plan(route, n_experts, tm):
    t = route.shape[0]
    ids = route[:, 2:2 + TOP_K].astype(jnp.int32).reshape(-1)
    onehot = (ids[:, None] == jnp.arange(n_experts, dtype=jnp.int32)[None, :]).astype(jnp.int32)
    csum = jnp.cumsum(onehot, axis=0)
    rank = jnp.sum((csum - onehot) * onehot, axis=1)
    counts = csum[-1]
    gsize = (counts + tm - 1) // tm * tm
    gend = jnp.cumsum(gsize)
    gstart = gend - gsize
    dest = gstart[ids] + rank
    n_tiles = (TOP_K * t) // tm + n_experts
    n_rows = n_tiles * tm
    pair = jnp.full((n_rows,), -1, jnp.int32).at[dest].set(jnp.arange(TOP_K * t, dtype=jnp.int32))
    row = jnp.arange(n_rows, dtype=jnp.int32)
    tok = jnp.where(pair >= 0, pair // TOP_K, 0)
    ydst = jnp.where(pair >= 0, (pair % TOP_K) * t + pair // TOP_K, TOP_K * t + row % tm)
    tile_start = jnp.arange(n_tiles, dtype=jnp.int32) * tm
    tile_valid = (tile_start < gend[-1]).astype(jnp.int32)
    tile_expert = jnp.minimum(jnp.sum((tile_start[:, None] >= gend[None, :]).astype(jnp.int32), axis=1),
                              n_experts - 1)
    last_expert = tile_expert[jnp.maximum(jnp.sum(tile_valid) - 1, 0)]
    tile_expert = jnp.where(tile_valid == 1, tile_expert, last_expert)
    spare = TOP_K * t + jnp.arange(tm, dtype=jnp.int32)
    ydst = jnp.concatenate([spare[None, :], ydst.reshape(n_tiles, tm)], axis=0)
    return tok.reshape(n_tiles, tm), ydst, tile_expert, tile_valid


def _moe_ffn_kernel(te_ref, tv_ref, tok_hbm, ydst_hbm, h_hbm, wg_ref, wu_ref, wd_ref, y_hbm,
                    x_buf, x_bf, acc, y_buf, tok_smem, ydst_smem, sem_idx, sem_gather, sem_scatter,
                    *, n_tiles, nf):
    r = pl.program_id(0)
    f = pl.program_id(1)
    tm = acc.shape[0]
    n_lane_tiles = acc.shape[1] // LANES
    share = tm // nf
    valid = tv_ref[r] == 1

    def idx_copies(tile):
        src_tile = jnp.minimum(tile, n_tiles - 1)
        dst = pl.ds(pl.multiple_of((tile & 3) * tm, tm), tm)
        return (pltpu.make_async_copy(tok_hbm.at[src_tile], tok_smem.at[dst], sem_idx.at[0]),
                pltpu.make_async_copy(ydst_hbm.at[src_tile], ydst_smem.at[dst], sem_idx.at[1]))

    def row_tile(buf, k):
        return buf.at[pl.ds(pl.multiple_of(k * n_lane_tiles, n_lane_tiles), n_lane_tiles), :]

    def gather_wait():
        pltpu.make_async_copy(x_buf, x_buf, sem_gather).wait()

    def scatter_wait():
        pltpu.make_async_copy(y_buf, y_buf, sem_scatter).wait()

    def start_gather(slot, k):
        pltpu.make_async_copy(h_hbm.at[tok_smem[slot * tm + k]], row_tile(x_buf, k), sem_gather).start()

    def start_scatter(slot, k):
        pltpu.make_async_copy(row_tile(y_buf, k), y_hbm.at[ydst_smem[slot * tm + k]], sem_scatter).start()

    def issue_share():
        nxt = (r + 1) & 3
        cur = r & 3
        for j in range(share):
            k = f * share + j
            start_gather(nxt, k)
            start_scatter(cur, k)

    @pl.when((r == 0) & (f == 0))
    def _first():
        acc[...] = jnp.zeros(acc.shape, F32)
        for c in idx_copies(0):
            c.start()
        for c in idx_copies(0):
            c.wait()

        def issue(k, carry):
            start_gather(0, k)
            return carry

        lax.fori_loop(0, tm, issue, 0, unroll=8)
        for c in idx_copies(1):
            c.start()

    @pl.when(f == 0)
    def _tile_start():
        @pl.when(r >= 1)
        def _():
            scatter_wait()

        for j in range(n_lane_tiles):
            y_buf[pl.ds(j, tm, stride=n_lane_tiles), :] = acc[:, j * LANES:(j + 1) * LANES]
        gather_wait()
        for j in range(n_lane_tiles):
            x_bf[:, j * LANES:(j + 1) * LANES] = x_buf[pl.ds(j, tm, stride=n_lane_tiles), :].astype(BF16)
        for c in idx_copies(r + 1):
            c.wait()
        for c in idx_copies(r + 2):
            c.start()

    @pl.when(valid)
    def _compute():
        issue_share()
        x = x_bf[...]
        g = jnp.dot(x, wg_ref[...], preferred_element_type=F32)
        u = jnp.dot(x, wu_ref[...], preferred_element_type=F32)
        a = (g * jax.nn.sigmoid(g) * u).astype(BF16)
        part = jnp.dot(a, wd_ref[...], preferred_element_type=F32)

        @pl.when(f == 0)
        def _():
            acc[...] = part

        @pl.when(f > 0)
        def _():
            acc[...] += part

    @pl.when(jnp.logical_not(valid))
    def _idle():
        issue_share()

    @pl.when((r == n_tiles - 1) & (f == nf - 1))
    def _drain():
        gather_wait()
        scatter_wait()
        for c in idx_copies(r + 2):
            c.wait()


def _moe_ffn(h2, tok, ydst, tile_expert, tile_valid, wg, wu, wd, tf=896):
    d = wg.shape[1]
    n_lane_tiles = d // LANES
    t = h2.shape[0] // n_lane_tiles
    h2 = h2.reshape(t, n_lane_tiles, LANES)
    n_tiles, tm = tok.shape
    ff = wg.shape[2]
    nf = ff // tf
    assert tm % nf == 0 and ydst.shape == (n_tiles + 1, tm)
    fsel = lambda f, tv, r: jnp.where(tv[r] == 1, f, nf - 1)
    grid_spec = pltpu.PrefetchScalarGridSpec(
        num_scalar_prefetch=2,
        grid=(n_tiles, nf),
        in_specs=[pl.BlockSpec(memory_space=pl.ANY), pl.BlockSpec(memory_space=pl.ANY),
                  pl.BlockSpec(memory_space=pl.ANY),
                  pl.BlockSpec((None, d, tf), lambda r, f, te, tv: (te[r], 0, fsel(f, tv, r))),
                  pl.BlockSpec((None, d, tf), lambda r, f, te, tv: (te[r], 0, fsel(f, tv, r))),
                  pl.BlockSpec((None, tf, d), lambda r, f, te, tv: (te[r], fsel(f, tv, r), 0))],
        out_specs=pl.BlockSpec(memory_space=pl.ANY),
        scratch_shapes=[pltpu.VMEM((tm * n_lane_tiles, LANES), F32), pltpu.VMEM((tm, d), BF16),
                        pltpu.VMEM((tm, d), F32), pltpu.VMEM((tm * n_lane_tiles, LANES), F32),
                        pltpu.SMEM((4 * tm,), jnp.int32), pltpu.SMEM((4 * tm,), jnp.int32),
                        pltpu.SemaphoreType.DMA((2,)), pltpu.SemaphoreType.DMA, pltpu.SemaphoreType.DMA],
    )
    return pl.pallas_call(
        functools.partial(_moe_ffn_kernel, n_tiles=n_tiles, nf=nf),
        grid_spec=grid_spec,
        out_shape=jax.ShapeDtypeStruct((TOP_K * t + tm, n_lane_tiles, LANES), F32),
        compiler_params=_cparams(("arbitrary", "arbitrary")),
        name="moe_ffn",
    )(tile_expert, tile_valid, tok, ydst, h2, wg, wu, wd)


def _combine_kernel(x_ref, y0_ref, y1_ref, route_ref, o_ref):
    route = route_ref[...]
    lane = lax.broadcasted_iota(jnp.int32, route.shape, 1)
    w0 = jnp.sum(jnp.where(lane == 0, route, 0.0), axis=1, keepdims=True)
    w1 = jnp.sum(jnp.where(lane == 1, route, 0.0), axis=1, keepdims=True)
    tm, d = x_ref.shape
    n_lane_tiles = d // LANES
    for j in range(n_lane_tiles):
        cols = slice(j * LANES, (j + 1) * LANES)
        rows = pl.ds(j, tm, stride=n_lane_tiles)
        o_ref[:, cols] = x_ref[:, cols] + w0 * y0_ref[rows, :] + w1 * y1_ref[rows, :]


def _combine(x1, y, route, tm=512):
    t, d = x1.shape
    nblk = t // tm
    return pl.pallas_call(
        _combine_kernel,
        grid=(nblk,),
        in_specs=[pl.BlockSpec((tm, d), lambda i: (i, 0)),
                  pl.BlockSpec((tm * (d // LANES), LANES), lambda i: (i, 0)),
                  pl.BlockSpec((tm * (d // LANES), LANES), lambda i: (i + nblk, 0)),
                  pl.BlockSpec((tm, LANES), lambda i: (i, 0))],
        out_specs=pl.BlockSpec((tm, d), lambda i: (i, 0)),
        out_shape=jax.ShapeDtypeStruct((t, d), F32),
        compiler_params=_cparams(("parallel",)),
        name="moe_combine",
    )(x1, y, y, route)


def _moe(h2, x1, route, wg, wu, wd, tm=1024):
    tok, ydst, tile_expert, tile_valid = _moe_plan(route, wg.shape[0], tm)
    y = _moe_ffn(h2, tok, ydst, tile_expert, tile_valid, wg, wu, wd)
    return _combine(x1, y.reshape(-1, LANES), route)


def _block_diag(w):
    nb, n, _ = w.shape
    eye = jnp.eye(nb, dtype=w.dtype)
    return (eye[:, None, :, None] * w[:, :, None, :]).reshape(nb * n, nb * n)


def kernel(x, norm_mix, w_in, conv_a_w, wo_a, q_norm, k_norm, wo_b, conv_c_w, conv_c_b, rg_w_r, rg_b_r, rg_w_i, rg_b_i, rg_lambda, wo_c, w_o, norm_ffn, ffn_w_gate, ffn_w_up, ffn_w_down, moe_router, moe_w_gate, moe_w_up, moe_w_down):
    bsz, s, d = x.shape
    depth = w_in.shape[0]
    cos_t, sa_t, sb_t = _rotary_tables(s)
    x2 = x.reshape(bsz * s, d)
    for l in range(depth):
        p2 = _inproj(x2, norm_mix[l][None, :], w_in[l].astype(BF16))
        p3 = p2.reshape(bsz, s, p2.shape[1])
        ya, yc = _seqmix(p3, conv_a_w[l], conv_c_w[l], conv_c_b[l][None, :],
                         _block_diag(rg_w_r[l]).astype(BF16), rg_b_r[l][None, :],
                         _block_diag(rg_w_i[l]).astype(BF16), rg_b_i[l][None, :],
                         rg_lambda[l][None, :])
        yb = _attention(p3, cos_t, sa_t, sb_t,
                        jnp.tile(q_norm[l], HEADS_PER_TILE)[None, :],
                        jnp.tile(k_norm[l], HEADS_PER_TILE)[None, :])
        cw = ya.shape[2]
        args = (x2, ya.reshape(bsz * s, cw), yb.reshape(bsz * s, yb.shape[2]), yc.reshape(bsz * s, cw), p2,
                wo_a[l].astype(BF16), wo_b[l].astype(BF16), wo_c[l].astype(BF16), w_o[l].astype(BF16),
                norm_ffn[l][None, :])
        if l % 2 == 0:
            x1, h2 = _merge(*args)
            x2 = _ffn(h2, x1, ffn_w_gate[l // 2].astype(BF16), ffn_w_up[l // 2].astype(BF16),
                      ffn_w_down[l // 2].astype(BF16))
        else:
            x1, h2, route = _merge(*args, router=moe_router[l // 2])
            x2 = _moe(h2, x1, route, moe_w_gate[l // 2].astype(BF16), moe_w_up[l // 2].astype(BF16),
                      moe_w_down[l // 2].astype(BF16))
    return x2.reshape(bsz, s, d)
```

```python
import functools
import math

import jax
import jax.numpy as jnp
from jax import lax
from jax.experimental import pallas as pl
from jax.experimental.pallas import tpu as pltpu

F32 = jnp.float32
BF16 = jnp.bfloat16

N_HEADS = 8
HEAD_DIM = 64
ROT_DIM = HEAD_DIM // 4
ROPE_THETA = 500000.0
MOBA_BLOCK = 256
MOBA_TOPK = 3
LRU_C = 8.0
N_BRANCHES = 3
TOP_K = 2
EPS = 1e-6
NEG = -1e30

LANES = 128
SUBLANES = 8
HEADS_PER_TILE = LANES // HEAD_DIM
VMEM_LIMIT = 56 * 1024 * 1024


def _cparams(sem):
    return pltpu.CompilerParams(dimension_semantics=sem, vmem_limit_bytes=VMEM_LIMIT)


def _split_bf16(x):
    hi = x.astype(BF16)
    lo = (x - hi.astype(F32)).astype(BF16)
    return hi, lo


def _rms(x, g):
    ms = jnp.mean(x * x, axis=-1, keepdims=True)
    return x * lax.rsqrt(ms + EPS) * g


def _inproj_kernel(x_ref, g_ref, w_ref, o_ref, *, tn):
    h = _rms(x_ref[...], g_ref[...]).astype(BF16)
    for c in range(o_ref.shape[1] // tn):
        cols = slice(c * tn, (c + 1) * tn)
        o_ref[:, cols] = jnp.dot(h, w_ref[:, cols], preferred_element_type=F32).astype(o_ref.dtype)


def _inproj(x2, g, w, tm=512, tn=1024):
    t, d = x2.shape
    n = w.shape[1]
    return pl.pallas_call(
        functools.partial(_inproj_kernel, tn=tn),
        grid=(t // tm,),
        in_specs=[pl.BlockSpec((tm, d), lambda i: (i, 0)),
                  pl.BlockSpec((1, d), lambda i: (0, 0)),
                  pl.BlockSpec((d, n), lambda i: (0, 0), pipeline_mode=pl.Buffered(1))],
        out_specs=pl.BlockSpec((tm, n), lambda i: (i, 0)),
        out_shape=jax.ShapeDtypeStruct((t, n), BF16),
        compiler_params=_cparams(("parallel",)),
        name="inproj",
    )(x2, g, w)


HALO = 8


def _seqmix_kernel(xa_ref, ba_ref, ca_ref, xc_ref, gc_ref, wa_ref, wc_ref, bc_ref,
                   wr_ref, br_ref, wi_ref, bi_ref, lam_ref, ya_ref, yc_ref,
                   ua_buf, xc_buf, h_carry):
    ts = xa_ref.shape[0]

    @pl.when(pl.program_id(1) == 0)
    def _():
        ua_buf[...] = jnp.zeros(ua_buf.shape, F32)
        xc_buf[...] = jnp.zeros(xc_buf.shape, F32)
        h_carry[...] = jnp.zeros_like(h_carry)

    def causal_conv(u, w_ref, tail_ref):
        k_w = w_ref.shape[0]
        row = lax.broadcasted_iota(jnp.int32, u.shape, 0)
        prev = jnp.concatenate([tail_ref[...]] * (ts // HALO), axis=0)
        y = w_ref[k_w - 1:k_w, :] * u
        for d in range(1, k_w):
            shifted = jnp.where(row >= d, pltpu.roll(u, d, 0), pltpu.roll(prev, d, 0))
            y = y + w_ref[k_w - 1 - d:k_w - d, :] * shifted
        tail_ref[...] = u[ts - HALO:ts, :]
        return y

    conv_a = causal_conv(ca_ref[...].astype(F32) * xa_ref[...].astype(F32), wa_ref, ua_buf)
    ya_ref[...] = (ba_ref[...].astype(F32) * conv_a).astype(ya_ref.dtype)

    xcv = causal_conv(xc_ref[...].astype(F32), wc_ref, xc_buf) + bc_ref[...]


    xb = xcv.astype(BF16)
    r = jax.nn.sigmoid(jnp.dot(xb, wr_ref[...], preferred_element_type=F32) + br_ref[...])
    ig = jax.nn.sigmoid(jnp.dot(xb, wi_ref[...], preferred_element_type=F32) + bi_ref[...])
    lam = lam_ref[...]
    softplus_neg_lam = jnp.maximum(-lam, 0.0) + jnp.log1p(jnp.exp(-jnp.abs(lam)))
    log_a = (-LRU_C) * r * softplus_neg_lam
    a = jnp.exp(log_a)
    u = jnp.sqrt(-jnp.tanh(log_a) * (1.0 + a * a)) * (ig * xcv)

    n_groups = ts // SUBLANES
    a = a.reshape(n_groups, SUBLANES, a.shape[1])
    u = u.reshape(n_groups, SUBLANES, u.shape[1])
    sub = lax.broadcasted_iota(jnp.int32, a.shape, 1)
    d = 1
    while d < SUBLANES:
        a_sh = jnp.where(sub >= d, pltpu.roll(a, d, 1), 1.0)
        u_sh = jnp.where(sub >= d, pltpu.roll(u, d, 1), 0.0)
        u = a * u_sh + u
        a = a * a_sh
        d *= 2
    carry = h_carry[...]
    groups = []
    for g in range(n_groups):
        hg = a[g] * carry + u[g]
        carry = hg[SUBLANES - 1:SUBLANES, :]
        groups.append(hg)
    h = jnp.concatenate(groups, axis=0)
    h_carry[...] = carry

    gc = gc_ref[...].astype(F32)
    gelu = 0.5 * gc * (1.0 + jnp.tanh(math.sqrt(2.0 / math.pi) * (gc + 0.044715 * (gc * gc * gc))))
    yc_ref[...] = (gelu * h).astype(yc_ref.dtype)


def _seqmix(p3, wa, wc, bc, wr_bd, br, wi_bd, bi, lam, ts=256):
    b, s, _ = p3.shape
    cw = wa.shape[1]
    col = lambda c: pl.BlockSpec((None, ts, cw), lambda bi_, si: (bi_, si, c))
    full = lambda arr: pl.BlockSpec(arr.shape, lambda bi_, si: (0,) * arr.ndim)
    out = jax.ShapeDtypeStruct((b, s, cw), BF16)
    ospec = pl.BlockSpec((None, ts, cw), lambda bi_, si: (bi_, si, 0))
    return pl.pallas_call(
        _seqmix_kernel,
        grid=(b, s // ts),
        in_specs=[col(0), col(1), col(2), col(6), col(7),
                  full(wa), full(wc), full(bc), full(wr_bd), full(br), full(wi_bd), full(bi), full(lam)],
        out_specs=[ospec, ospec],
        out_shape=[out, out],
        scratch_shapes=[pltpu.VMEM((HALO, cw), F32), pltpu.VMEM((HALO, cw), F32),
                        pltpu.VMEM((1, cw), F32)],
        compiler_params=_cparams(("parallel", "arbitrary")),
        name="seqmix",
    )(p3, p3, p3, p3, p3, wa, wc, bc, wr_bd, br, wi_bd, bi, lam)


def _dot_nt(a, b):
    return lax.dot_general(a, b, (((1,), (1,)), ((), ())), preferred_element_type=F32)


def _attn_kernel(q_ref, k_ref, v_ref, cos_ref, sa_ref, sb_ref, qg_ref, kg_ref, o_ref,
                 qa_scr, kaug, vaug):
    s_len = k_ref.shape[0]
    blk = MOBA_BLOCK
    nb = s_len // blk

    def _prep():
        li = lax.broadcasted_iota(jnp.int32, (LANES, LANES), 0) // HEAD_DIM
        lj = lax.broadcasted_iota(jnp.int32, (LANES, LANES), 1) // HEAD_DIM
        head_ones = jnp.where(li == lj, 1.0, 0.0).astype(BF16)

        def norm_rot(t_ref, gain):
            t = t_ref[...].astype(F32)
            hi, lo = _split_bf16(t * t)
            ss = (jnp.dot(hi, head_ones, preferred_element_type=F32)
                  + jnp.dot(lo, head_ones, preferred_element_type=F32))
            tn = t * lax.rsqrt(ss * (1.0 / HEAD_DIM) + EPS) * gain
            return (tn * cos_ref[...] + pltpu.roll(tn, LANES - ROT_DIM // 2, 1) * sa_ref[...]
                    + pltpu.roll(tn, ROT_DIM // 2, 1) * sb_ref[...])

        qr = norm_rot(q_ref, qg_ref[...]) * (HEAD_DIM ** -0.5 * math.log2(math.e))
        kr = norm_rot(k_ref, kg_ref[...])
        lane = lax.broadcasted_iota(jnp.int32, (s_len, LANES), 1)
        kblk = lax.broadcasted_iota(jnp.int32, (s_len, LANES), 0) // blk
        kaug[0] = jnp.where(lane < HEAD_DIM, kr,
                            jnp.where(lane - HEAD_DIM == kblk, 1.0, 0.0)).astype(BF16)
        kaug[1] = jnp.where(lane >= HEAD_DIM, kr,
                            jnp.where(lane == kblk, 1.0, 0.0)).astype(BF16)
        v = v_ref[...]
        vaug[0] = jnp.where(lane < HEAD_DIM, v, jnp.ones_like(v))
        vaug[1] = jnp.where(lane >= HEAD_DIM, v, jnp.ones_like(v))
        kmean = jnp.sum(kr.reshape(nb, blk, LANES), axis=1) * (1.0 / blk)
        ml = lax.broadcasted_iota(jnp.int32, (nb, LANES), 1)
        km0 = jnp.where(ml < HEAD_DIM, kmean, 0.0)
        km1 = jnp.where(ml >= HEAD_DIM, kmean, 0.0)
        pad = jnp.zeros((HEAD_DIM - nb, LANES), F32)
        wg = jnp.concatenate([km1, pad, km0, pad], axis=0)
        w_hi, w_lo = _split_bf16(wg)
        q_hi, q_lo = _split_bf16(qr)
        gate_t = _dot_nt(w_hi, q_hi) + _dot_nt(w_lo, q_hi) + _dot_nt(w_hi, q_lo)
        n = lax.broadcasted_iota(jnp.int32, (nb, s_len), 0)
        qblk = lax.broadcasted_iota(jnp.int32, (nb, s_len), 1) // blk

        def select_bias_t(g_rows):
            g = jnp.where(n < qblk, g_rows, -jnp.inf)
            bias = jnp.where(n == qblk, 0.0, NEG)
            for k in range(MOBA_TOPK):
                m = jnp.max(g, axis=0, keepdims=True)
                idx = jnp.min(jnp.where(g == m, n, 1 << 30), axis=0, keepdims=True)
                hit = n == idx
                bias = jnp.where(hit & (k < qblk), 0.0, bias)
                g = jnp.where(hit, -jnp.inf, g)
            return bias

        pad_t = jnp.zeros((HEAD_DIM - nb, s_len), F32)
        bias_t = jnp.concatenate([select_bias_t(gate_t[0:nb]), pad_t,
                                  select_bias_t(gate_t[HEAD_DIM:HEAD_DIM + nb]), pad_t], axis=0)
        bias = bias_t.T
        qa_scr[0] = jnp.where(lane < HEAD_DIM, qr, bias).astype(BF16)
        qa_scr[1] = jnp.where(lane >= HEAD_DIM, qr, bias).astype(BF16)

    _prep()
    lane_q = lax.broadcasted_iota(jnp.int32, (blk, LANES), 1)

    def query_block(nblk):
        width = nblk * blk
        row0 = (nblk - 1) * blk
        tri = (lax.broadcasted_iota(jnp.int32, (blk, blk), 1)
               <= lax.broadcasted_iota(jnp.int32, (blk, blk), 0))
        out = []
        for h in range(HEADS_PER_TILE):
            s = _dot_nt(qa_scr[h, row0:row0 + blk, :], kaug[h, 0:width, :])
            own = jnp.where(tri, s[:, width - blk:], NEG)
            s = own if nblk == 1 else jnp.concatenate([s[:, :width - blk], own], axis=1)
            mx = s[:, 0:LANES]
            for t in range(1, width // LANES):
                mx = jnp.maximum(mx, s[:, t * LANES:(t + 1) * LANES])
            m = jnp.broadcast_to(jnp.max(mx, axis=1, keepdims=True), (blk, LANES))
            p = jnp.exp2(s - jnp.concatenate([m] * (width // LANES), axis=1)).astype(BF16)
            out.append(jnp.dot(p, vaug[h, 0:width, :], preferred_element_type=F32))
        a0, a1 = out
        o_ref[row0:row0 + blk, :] = jnp.where(lane_q < HEAD_DIM, a0 / pltpu.roll(a0, HEAD_DIM, 1),
                                              a1 / pltpu.roll(a1, HEAD_DIM, 1)).astype(o_ref.dtype)

    def one_block(i, carry):
        for nblk in range(1, nb + 1):
            pl.when(i == nblk - 1)(functools.partial(query_block, nblk))
        return carry

    lax.fori_loop(0, nb, one_block, 0)


def _attention(p3, cos_t, sa_t, sb_t, qg, kg):
    b, s, _ = p3.shape
    n_tiles = N_HEADS // HEADS_PER_TILE
    qcol, kcol, vcol = 12, 16, 20
    seq = lambda c0: pl.BlockSpec((None, s, LANES), lambda bi, hp: (bi, 0, c0 + hp))
    tab = pl.BlockSpec((s, LANES), lambda bi, hp: (0, 0))
    gain = pl.BlockSpec((1, LANES), lambda bi, hp: (0, 0))
    return pl.pallas_call(
        _attn_kernel,
        grid=(b, n_tiles),
        in_specs=[seq(qcol), seq(kcol), seq(vcol), tab, tab, tab, gain, gain],
        out_specs=pl.BlockSpec((None, s, LANES), lambda bi, hp: (bi, 0, hp)),
        out_shape=jax.ShapeDtypeStruct((b, s, N_HEADS * HEAD_DIM), BF16),
        scratch_shapes=[pltpu.VMEM((HEADS_PER_TILE, s, LANES), BF16),
                        pltpu.VMEM((HEADS_PER_TILE, s, LANES), BF16),
                        pltpu.VMEM((HEADS_PER_TILE, s, LANES), BF16)],
        compiler_params=_cparams(("parallel", "parallel")),
        name="moba_attention",
    )(p3, p3, p3, cos_t, sa_t, sb_t, qg, kg)


def _rotary_tables(s):
    half = ROT_DIM // 2
    inv_freq = ROPE_THETA ** (-jnp.arange(0, ROT_DIM, 2, dtype=F32) / ROT_DIM)
    ang = jnp.arange(s, dtype=jnp.int32).astype(F32)[:, None] * inv_freq[None, :]
    cos, sin = jnp.cos(ang), jnp.sin(ang)
    ones = jnp.ones((s, HEAD_DIM - ROT_DIM), F32)
    zeros = jnp.zeros((s, HEAD_DIM - ROT_DIM), F32)
    zh = jnp.zeros((s, half), F32)
    cos_h = jnp.concatenate([cos, cos, ones], axis=1)
    sa_h = jnp.concatenate([-sin, zh, zeros], axis=1)
    sb_h = jnp.concatenate([zh, sin, zeros], axis=1)
    tile = lambda t: jnp.tile(t, (1, HEADS_PER_TILE))
    return tile(cos_h), tile(sa_h), tile(sb_h)


def _route(logits, n_experts):
    lane = lax.broadcasted_iota(jnp.int32, logits.shape, 1)
    g = jnp.where(lane < n_experts, logits, -jnp.inf)
    m1 = jnp.max(g, axis=1, keepdims=True)
    i1 = jnp.min(jnp.where(g == m1, lane, 1 << 30), axis=1, keepdims=True)
    g2 = jnp.where(lane == i1, -jnp.inf, g)
    m2 = jnp.max(g2, axis=1, keepdims=True)
    i2 = jnp.min(jnp.where(g2 == m2, lane, 1 << 30), axis=1, keepdims=True)
    e2 = jnp.exp(m2 - m1)
    w1 = 1.0 / (1.0 + e2)
    w2 = e2 / (1.0 + e2)
    return jnp.where(lane == 0, w1, jnp.where(lane == 1, w2, jnp.where(
        lane == 2, i1.astype(F32), jnp.where(lane == 3, i2.astype(F32), 0.0))))


def _merge_kernel(*refs, with_router, n_experts):
    if with_router:
        (x_ref, ya_ref, yb_ref, yc_ref, ga_ref, gb_ref, gcg_ref, woa_ref, wob_ref, woc_ref, wo_ref,
         gn_ref, rhi_ref, rlo_ref, x1_ref, h2_ref, route_ref) = refs
    else:
        (x_ref, ya_ref, yb_ref, yc_ref, ga_ref, gb_ref, gcg_ref, woa_ref, wob_ref, woc_ref, wo_ref,
         gn_ref, x1_ref, h2_ref) = refs
    merged = (jax.nn.sigmoid(ga_ref[...].astype(F32))
              * jnp.dot(ya_ref[...], woa_ref[...], preferred_element_type=F32)
              + jax.nn.sigmoid(gb_ref[...].astype(F32))
              * jnp.dot(yb_ref[...], wob_ref[...], preferred_element_type=F32)
              + jax.nn.sigmoid(gcg_ref[...].astype(F32))
              * jnp.dot(yc_ref[...], woc_ref[...], preferred_element_type=F32))
    x1 = x_ref[...] + jnp.dot(merged.astype(BF16), wo_ref[...], preferred_element_type=F32)
    x1_ref[...] = x1
    h2 = _rms(x1, gn_ref[...])
    if with_router:
        n_lane_tiles = h2.shape[1] // LANES
        for j in range(n_lane_tiles):
            h2_ref[pl.ds(j, h2.shape[0], stride=n_lane_tiles), :] = h2[:, j * LANES:(j + 1) * LANES]
        h_hi, h_lo = _split_bf16(h2)
        logits = (jnp.dot(h_hi, rhi_ref[...], preferred_element_type=F32)
                  + jnp.dot(h_hi, rlo_ref[...], preferred_element_type=F32)
                  + jnp.dot(h_lo, rhi_ref[...], preferred_element_type=F32))
        route_ref[...] = _route(logits, n_experts)
    else:
        h2_ref[...] = h2.astype(BF16)


def _merge(x2, ya, yb, yc, p2, woa, wob, woc, wo, gn, router=None, tm=512):
    t, d = x2.shape
    cw = ya.shape[1]
    row = lambda w: pl.BlockSpec((tm, w), lambda i: (i, 0))
    full = lambda arr: pl.BlockSpec(arr.shape, lambda i: (0,) * arr.ndim)
    gate_col0 = (p2.shape[1] - N_BRANCHES * d) // d
    gspec = lambda c: pl.BlockSpec((tm, d), lambda i: (i, gate_col0 + c))
    in_specs = [row(d), row(cw), row(cw), row(cw), gspec(0), gspec(1), gspec(2),
                full(woa), full(wob), full(woc), full(wo), full(gn)]
    args = [x2, ya, yb, yc, p2, p2, p2, woa, wob, woc, wo, gn]
    out_specs = [row(d), row(d)]
    out_shape = [jax.ShapeDtypeStruct((t, d), F32), jax.ShapeDtypeStruct((t, d), BF16)]
    n_experts = 0
    if router is not None:
        n_experts = router.shape[1]
        rpad = jnp.zeros((d, LANES), F32).at[:, :n_experts].set(router)
        r_hi, r_lo = _split_bf16(rpad)
        in_specs += [full(r_hi), full(r_lo)]
        args += [r_hi, r_lo]
        out_specs[1] = pl.BlockSpec((tm * (d // LANES), LANES), lambda i: (i, 0))
        out_shape[1] = jax.ShapeDtypeStruct((t * (d // LANES), LANES), F32)
        out_specs.append(row(LANES))
        out_shape.append(jax.ShapeDtypeStruct((t, LANES), F32))
    return pl.pallas_call(
        functools.partial(_merge_kernel, with_router=router is not None, n_experts=n_experts),
        grid=(t // tm,),
        in_specs=in_specs, out_specs=out_specs, out_shape=out_shape,
        compiler_params=_cparams(("parallel",)),
        name="merge",
    )(*args)


def _ffn_kernel(h_ref, x_ref, wg_ref, wu_ref, wd_ref, o_ref, *, tf):
    h = h_ref[...]
    out = x_ref[...]
    for c in range(wg_ref.shape[1] // tf):
        cols = slice(c * tf, (c + 1) * tf)
        g = jnp.dot(h, wg_ref[:, cols], preferred_element_type=F32)
        u = jnp.dot(h, wu_ref[:, cols], preferred_element_type=F32)
        a = (g * jax.nn.sigmoid(g) * u).astype(BF16)
        out = out + jnp.dot(a, wd_ref[cols, :], preferred_element_type=F32)
    o_ref[...] = out


def _ffn(h2, x1, wg, wu, wd, tm=512, tf=1408):
    t, d = x1.shape
    ff = wg.shape[1]
    resident = lambda arr: pl.BlockSpec(arr.shape, lambda i: (0, 0), pipeline_mode=pl.Buffered(1))
    return pl.pallas_call(
        functools.partial(_ffn_kernel, tf=tf),
        grid=(t // tm,),
        in_specs=[pl.BlockSpec((tm, d), lambda i: (i, 0)),
                  pl.BlockSpec((tm, d), lambda i: (i, 0)),
                  resident(wg), resident(wu), resident(wd)],
        out_specs=pl.BlockSpec((tm, d), lambda i: (i, 0)),
        out_shape=jax.ShapeDtypeStruct((t, d), F32),
        compiler_params=_cparams(("parallel",)),
        name="dense_ffn",
    )(h2, x1, wg, wu, wd)


def _moe_plan(route, n_experts, tm):
    t = route.shape[0]
    ids = route[:, 2:2 + TOP_K].astype(jnp.int32).reshape(-1)
    onehot = (ids[:, None] == jnp.arange(n_experts, dtype=jnp.int32)[None, :]).astype(jnp.int32)
    csum = jnp.cumsum(onehot, axis=0)
    rank = jnp.sum((csum - onehot) * onehot, axis=1)
    counts = csum[-1]
    gsize = (counts + tm - 1) // tm * tm
    gend = jnp.cumsum(gsize)
    gstart = gend - gsize
    dest = gstart[ids] + rank
    n_tiles = (TOP_K * t) // tm + n_experts
    n_rows = n_tiles * tm
    pair = jnp.full((n_rows,), -1, jnp.int32).at[dest].set(jnp.arange(TOP_K * t, dtype=jnp.int32))
    row = jnp.arange(n_rows, dtype=jnp.int32)
    tok = jnp.where(pair >= 0, pair // TOP_K, 0)
    ydst = jnp.where(pair >= 0, (pair % TOP_K) * t + pair // TOP_K, TOP_K * t + row % tm)
    tile_start = jnp.arange(n_tiles, dtype=jnp.int32) * tm
    tile_valid = (tile_start < gend[-1]).astype(jnp.int32)
    tile_expert = jnp.minimum(jnp.sum((tile_start[:, None] >= gend[None, :]).astype(jnp.int32), axis=1),
                              n_experts - 1)
    last_expert = tile_expert[jnp.maximum(jnp.sum(tile_valid) - 1, 0)]
    tile_expert = jnp.where(tile_valid == 1, tile_expert, last_expert)
    spare = TOP_K * t + jnp.arange(tm, dtype=jnp.int32)
    ydst = jnp.concatenate([spare[None, :], ydst.reshape(n_tiles, tm)], axis=0)
    return tok.reshape(n_tiles, tm), ydst, tile_expert, tile_valid


def _moe_ffn_kernel(te_ref, tv_ref, tok_hbm, ydst_hbm, h_hbm, wg_ref, wu_ref, wd_ref, y_hbm,
                    x_buf, x_bf, acc, y_buf, tok_smem, ydst_smem, sem_idx, sem_gather, sem_scatter,
                    *, n_tiles, nf):
    r = pl.program_id(0)
    f = pl.program_id(1)
    tm = acc.shape[0]
    n_lane_tiles = acc.shape[1] // LANES
    share = tm // nf
    valid = tv_ref[r] == 1

    def idx_copies(tile):
        src_tile = jnp.minimum(tile, n_tiles - 1)
        dst = pl.ds(pl.multiple_of((tile & 3) * tm, tm), tm)
        return (pltpu.make_async_copy(tok_hbm.at[src_tile], tok_smem.at[dst], sem_idx.at[0]),
                pltpu.make_async_copy(ydst_hbm.at[src_tile], ydst_smem.at[dst], sem_idx.at[1]))

    def row_tile(buf, k):
        return buf.at[pl.ds(pl.multiple_of(k * n_lane_tiles, n_lane_tiles), n_lane_tiles), :]

    def gather_wait():
        pltpu.make_async_copy(x_buf, x_buf, sem_gather).wait()

    def scatter_wait():
        pltpu.make_async_copy(y_buf, y_buf, sem_scatter).wait()

    def start_gather(slot, k):
        pltpu.make_async_copy(h_hbm.at[tok_smem[slot * tm + k]], row_tile(x_buf, k), sem_gather).start()

    def start_scatter(slot, k):
        pltpu.make_async_copy(row_tile(y_buf, k), y_hbm.at[ydst_smem[slot * tm + k]], sem_scatter).start()

    def issue_share():
        nxt = (r + 1) & 3
        cur = r & 3
        for j in range(share):
            k = f * share + j
            start_gather(nxt, k)
            start_scatter(cur, k)

    @pl.when((r == 0) & (f == 0))
    def _first():
        acc[...] = jnp.zeros(acc.shape, F32)
        for c in idx_copies(0):
            c.start()
        for c in idx_copies(0):
            c.wait()

        def issue(k, carry):
            start_gather(0, k)
            return carry

        lax.fori_loop(0, tm, issue, 0, unroll=8)
        for c in idx_copies(1):
            c.start()

    @pl.when(f == 0)
    def _tile_start():
        @pl.when(r >= 1)
        def _():
            scatter_wait()

        for j in range(n_lane_tiles):
            y_buf[pl.ds(j, tm, stride=n_lane_tiles), :] = acc[:, j * LANES:(j + 1) * LANES]
        gather_wait()
        for j in range(n_lane_tiles):
            x_bf[:, j * LANES:(j + 1) * LANES] = x_buf[pl.ds(j, tm, stride=n_lane_tiles), :].astype(BF16)
        for c in idx_copies(r + 1):
            c.wait()
        for c in idx_copies(r + 2):
            c.start()

    @pl.when(valid)
    def _compute():
        issue_share()
        x = x_bf[...]
        g = jnp.dot(x, wg_ref[...], preferred_element_type=F32)
        u = jnp.dot(x, wu_ref[...], preferred_element_type=F32)
        a = (g * jax.nn.sigmoid(g) * u).astype(BF16)
        part = jnp.dot(a, wd_ref[...], preferred_element_type=F32)

        @pl.when(f == 0)
        def _():
            acc[...] = part

        @pl.when(f > 0)
        def _():
            acc[...] += part

    @pl.when(jnp.logical_not(valid))
    def _idle():
        issue_share()

    @pl.when((r == n_tiles - 1) & (f == nf - 1))
    def _drain():
        gather_wait()
        scatter_wait()
        for c in idx_copies(r + 2):
            c.wait()


def _moe_ffn(h2, tok, ydst, tile_expert, tile_valid, wg, wu, wd, tf=896):
    d = wg.shape[1]
    n_lane_tiles = d // LANES
    t = h2.shape[0] // n_lane_tiles
    h2 = h2.reshape(t, n_lane_tiles, LANES)
    n_tiles, tm = tok.shape
    ff = wg.shape[2]
    nf = ff // tf
    assert tm % nf == 0 and ydst.shape == (n_tiles + 1, tm)
    fsel = lambda f, tv, r: jnp.where(tv[r] == 1, f, nf - 1)
    grid_spec = pltpu.PrefetchScalarGridSpec(
        num_scalar_prefetch=2,
        grid=(n_tiles, nf),
        in_specs=[pl.BlockSpec(memory_space=pl.ANY), pl.BlockSpec(memory_space=pl.ANY),
                  pl.BlockSpec(memory_space=pl.ANY),
                  pl.BlockSpec((None, d, tf), lambda r, f, te, tv: (te[r], 0, fsel(f, tv, r))),
                  pl.BlockSpec((None, d, tf), lambda r, f, te, tv: (te[r], 0, fsel(f, tv, r))),
                  pl.BlockSpec((None, tf, d), lambda r, f, te, tv: (te[r], fsel(f, tv, r), 0))],
        out_specs=pl.BlockSpec(memory_space=pl.ANY),
        scratch_shapes=[pltpu.VMEM((tm * n_lane_tiles, LANES), F32), pltpu.VMEM((tm, d), BF16),
                        pltpu.VMEM((tm, d), F32), pltpu.VMEM((tm * n_lane_tiles, LANES), F32),
                        pltpu.SMEM((4 * tm,), jnp.int32), pltpu.SMEM((4 * tm,), jnp.int32),
                        pltpu.SemaphoreType.DMA((2,)), pltpu.SemaphoreType.DMA, pltpu.SemaphoreType.DMA],
    )
    return pl.pallas_call(
        functools.partial(_moe_ffn_kernel, n_tiles=n_tiles, nf=nf),
        grid_spec=grid_spec,
        out_shape=jax.ShapeDtypeStruct((TOP_K * t + tm, n_lane_tiles, LANES), F32),
        compiler_params=_cparams(("arbitrary", "arbitrary")),
        name="moe_ffn",
    )(tile_expert, tile_valid, tok, ydst, h2, wg, wu, wd)


def _combine_kernel(x_ref, y0_ref, y1_ref, route_ref, o_ref):
    route = route_ref[...]
    lane = lax.broadcasted_iota(jnp.int32, route.shape, 1)
    w0 = jnp.sum(jnp.where(lane == 0, route, 0.0), axis=1, keepdims=True)
    w1 = jnp.sum(jnp.where(lane == 1, route, 0.0), axis=1, keepdims=True)
    tm, d = x_ref.shape
    n_lane_tiles = d // LANES
    for j in range(n_lane_tiles):
        cols = slice(j * LANES, (j + 1) * LANES)
        rows = pl.ds(j, tm, stride=n_lane_tiles)
        o_ref[:, cols] = x_ref[:, cols] + w0 * y0_ref[rows, :] + w1 * y1_ref[rows, :]


def _combine(x1, y, route, tm=512):
    t, d = x1.shape
    nblk = t // tm
    return pl.pallas_call(
        _combine_kernel,
        grid=(nblk,),
        in_specs=[pl.BlockSpec((tm, d), lambda i: (i, 0)),
                  pl.BlockSpec((tm * (d // LANES), LANES), lambda i: (i, 0)),
                  pl.BlockSpec((tm * (d // LANES), LANES), lambda i: (i + nblk, 0)),
                  pl.BlockSpec((tm, LANES), lambda i: (i, 0))],
        out_specs=pl.BlockSpec((tm, d), lambda i: (i, 0)),
        out_shape=jax.ShapeDtypeStruct((t, d), F32),
        compiler_params=_cparams(("parallel",)),
        name="moe_combine",
    )(x1, y, y, route)


def _moe(h2, x1, route, wg, wu, wd, tm=1024):
    tok, ydst, tile_expert, tile_valid = _moe_plan(route, wg.shape[0], tm)
    y = _moe_ffn(h2, tok, ydst, tile_expert, tile_valid, wg, wu, wd)
    return _combine(x1, y.reshape(-1, LANES), route)


def _block_diag(w):
    nb, n, _ = w.shape
    eye = jnp.eye(nb, dtype=w.dtype)
    return (eye[:, None, :, None] * w[:, :, None, :]).reshape(nb * n, nb * n)


def kernel(x, norm_mix, w_in, conv_a_w, wo_a, q_norm, k_norm, wo_b, conv_c_w, conv_c_b, rg_w_r, rg_b_r, rg_w_i, rg_b_i, rg_lambda, wo_c, w_o, norm_ffn, ffn_w_gate, ffn_w_up, ffn_w_down, moe_router, moe_w_gate, moe_w_up, moe_w_down):
    bsz, s, d = x.shape
    depth = w_in.shape[0]
    cos_t, sa_t, sb_t = _rotary_tables(s)
    x2 = x.reshape(bsz * s, d)
    for l in range(depth):
        p2 = _inproj(x2, norm_mix[l][None, :], w_in[l].astype(BF16))
        p3 = p2.reshape(bsz, s, p2.shape[1])
        ya, yc = _seqmix(p3, conv_a_w[l], conv_c_w[l], conv_c_b[l][None, :],
                         _block_diag(rg_w_r[l]).astype(BF16), rg_b_r[l][None, :],
                         _block_diag(rg_w_i[l]).astype(BF16), rg_b_i[l][None, :],
                         rg_lambda[l][None, :])
        yb = _attention(p3, cos_t, sa_t, sb_t,
                        jnp.tile(q_norm[l], HEADS_PER_TILE)[None, :],
                        jnp.tile(k_norm[l], HEADS_PER_TILE)[None, :])
        cw = ya.shape[2]
        args = (x2, ya.reshape(bsz * s, cw), yb.reshape(bsz * s, yb.shape[2]), yc.reshape(bsz * s, cw), p2,
                wo_a[l].astype(BF16), wo_b[l].astype(BF16), wo_c[l].astype(BF16), w_o[l].astype(BF16),
                norm_ffn[l][None, :])
        if l % 2 == 0:
            x1, h2 = _merge(*args)
            x2 = _ffn(h2, x1, ffn_w_gate[l // 2].astype(BF16), ffn_w_up[l // 2].astype(BF16),
                      ffn_w_down[l // 2].astype(BF16))
        else:
            x1, h2, route = _merge(*args, router=moe_router[l // 2])
            x2 = _moe(h2, x1, route, moe_w_gate[l // 2].astype(BF16), moe_w_up[l // 2].astype(BF16),
                      moe_w_down[l // 2].astype(BF16))
    return x2.reshape(bsz, s, d)
```

```python
import functools
import math

import jax
import jax.numpy as jnp
from jax import lax
from jax.experimental import pallas as pl
from jax.experimental.pallas import tpu as pltpu

F32 = jnp.float32
BF16 = jnp.bfloat16

N_HEADS = 8
HEAD_DIM = 64
ROT_DIM = HEAD_DIM // 4
ROPE_THETA = 500000.0
MOBA_BLOCK = 256
MOBA_TOPK = 3
LRU_C = 8.0
N_BRANCHES = 3
TOP_K = 2
EPS = 1e-6
NEG = -1e30

LANES = 128
SUBLANES = 8
HEADS_PER_TILE = LANES // HEAD_DIM
VMEM_LIMIT = 56 * 1024 * 1024


def _cparams(sem):
    return pltpu.CompilerParams(dimension_semantics=sem, vmem_limit_bytes=VMEM_LIMIT)


def _split_bf16(x):
    hi = x.astype(BF16)
    lo = (x - hi.astype(F32)).astype(BF16)
    return hi, lo


def _rms(x, g):
    ms = jnp.mean(x * x, axis=-1, keepdims=True)
    return x * lax.rsqrt(ms + EPS) * g


def _inproj_kernel(x_ref, g_ref, w_ref, o_ref, *, tn):
    h = _rms(x_ref[...], g_ref[...]).astype(BF16)
    for c in range(o_ref.shape[1] // tn):
        cols = slice(c * tn, (c + 1) * tn)
        o_ref[:, cols] = jnp.dot(h, w_ref[:, cols], preferred_element_type=F32).astype(o_ref.dtype)


def _inproj(x2, g, w, tm=512, tn=1024):
    t, d = x2.shape
    n = w.shape[1]
    return pl.pallas_call(
        functools.partial(_inproj_kernel, tn=tn),
        grid=(t // tm,),
        in_specs=[pl.BlockSpec((tm, d), lambda i: (i, 0)),
                  pl.BlockSpec((1, d), lambda i: (0, 0)),
                  pl.BlockSpec((d, n), lambda i: (0, 0), pipeline_mode=pl.Buffered(1))],
        out_specs=pl.BlockSpec((tm, n), lambda i: (i, 0)),
        out_shape=jax.ShapeDtypeStruct((t, n), BF16),
        compiler_params=_cparams(("parallel",)),
        name="inproj",
    )(x2, g, w)


HALO = 8


def _seqmix_kernel(xa_ref, ba_ref, ca_ref, xc_ref, gc_ref, wa_ref, wc_ref, bc_ref,
                   wr_ref, br_ref, wi_ref, bi_ref, lam_ref, ya_ref, yc_ref,
                   ua_buf, xc_buf, h_carry):
    ts = xa_ref.shape[0]

    @pl.when(pl.program_id(1) == 0)
    def _():
        ua_buf[...] = jnp.zeros(ua_buf.shape, F32)
        xc_buf[...] = jnp.zeros(xc_buf.shape, F32)
        h_carry[...] = jnp.zeros_like(h_carry)

    def causal_conv(u, w_ref, tail_ref):
        k_w = w_ref.shape[0]
        row = lax.broadcasted_iota(jnp.int32, u.shape, 0)
        prev = jnp.concatenate([tail_ref[...]] * (ts // HALO), axis=0)
        y = w_ref[k_w - 1:k_w, :] * u
        for d in range(1, k_w):
            shifted = jnp.where(row >= d, pltpu.roll(u, d, 0), pltpu.roll(prev, d, 0))
            y = y + w_ref[k_w - 1 - d:k_w - d, :] * shifted
        tail_ref[...] = u[ts - HALO:ts, :]
        return y

    conv_a = causal_conv(ca_ref[...].astype(F32) * xa_ref[...].astype(F32), wa_ref, ua_buf)
    ya_ref[...] = (ba_ref[...].astype(F32) * conv_a).astype(ya_ref.dtype)

    xcv = causal_conv(xc_ref[...].astype(F32), wc_ref, xc_buf) + bc_ref[...]


    xb = xcv.astype(BF16)
    r = jax.nn.sigmoid(jnp.dot(xb, wr_ref[...], preferred_element_type=F32) + br_ref[...])
    ig = jax.nn.sigmoid(jnp.dot(xb, wi_ref[...], preferred_element_type=F32) + bi_ref[...])
    lam = lam_ref[...]
    softplus_neg_lam = jnp.maximum(-lam, 0.0) + jnp.log1p(jnp.exp(-jnp.abs(lam)))
    log_a = (-LRU_C) * r * softplus_neg_lam
    a = jnp.exp(log_a)
    u = jnp.sqrt(-jnp.tanh(log_a) * (1.0 + a * a)) * (ig * xcv)

    n_groups = ts // SUBLANES
    a = a.reshape(n_groups, SUBLANES, a.shape[1])
    u = u.reshape(n_groups, SUBLANES, u.shape[1])
    sub = lax.broadcasted_iota(jnp.int32, a.shape, 1)
    d = 1
    while d < SUBLANES:
        a_sh = jnp.where(sub >= d, pltpu.roll(a, d, 1), 1.0)
        u_sh = jnp.where(sub >= d, pltpu.roll(u, d, 1), 0.0)
        u = a * u_sh + u
        a = a * a_sh
        d *= 2
    carry = h_carry[...]
    groups = []
    for g in range(n_groups):
        hg = a[g] * carry + u[g]
        carry = hg[SUBLANES - 1:SUBLANES, :]
        groups.append(hg)
    h = jnp.concatenate(groups, axis=0)
    h_carry[...] = carry

    gc = gc_ref[...].astype(F32)
    gelu = 0.5 * gc * (1.0 + jnp.tanh(math.sqrt(2.0 / math.pi) * (gc + 0.044715 * (gc * gc * gc))))
    yc_ref[...] = (gelu * h).astype(yc_ref.dtype)


def _seqmix(p3, wa, wc, bc, wr_bd, br, wi_bd, bi, lam, ts=256):
    b, s, _ = p3.shape
    cw = wa.shape[1]
    col = lambda c: pl.BlockSpec((None, ts, cw), lambda bi_, si: (bi_, si, c))
    full = lambda arr: pl.BlockSpec(arr.shape, lambda bi_, si: (0,) * arr.ndim)
    out = jax.ShapeDtypeStruct((b, s, cw), BF16)
    ospec = pl.BlockSpec((None, ts, cw), lambda bi_, si: (bi_, si, 0))
    return pl.pallas_call(
        _seqmix_kernel,
        grid=(b, s // ts),
        in_specs=[col(0), col(1), col(2), col(6), col(7),
                  full(wa), full(wc), full(bc), full(wr_bd), full(br), full(wi_bd), full(bi), full(lam)],
        out_specs=[ospec, ospec],
        out_shape=[out, out],
        scratch_shapes=[pltpu.VMEM((HALO, cw), F32), pltpu.VMEM((HALO, cw), F32),
                        pltpu.VMEM((1, cw), F32)],
        compiler_params=_cparams(("parallel", "arbitrary")),
        name="seqmix",
    )(p3, p3, p3, p3, p3, wa, wc, bc, wr_bd, br, wi_bd, bi, lam)


def _dot_nt(a, b):
    return lax.dot_general(a, b, (((1,), (1,)), ((), ())), preferred_element_type=F32)


def _attn_kernel(q_ref, k_ref, v_ref, cos_ref, sa_ref, sb_ref, qg_ref, kg_ref, o_ref,
                 qa_scr, kaug, vaug):
    s_len = k_ref.shape[0]
    blk = MOBA_BLOCK
    nb = s_len // blk

    def _prep():
        li = lax.broadcasted_iota(jnp.int32, (LANES, LANES), 0) // HEAD_DIM
        lj = lax.broadcasted_iota(jnp.int32, (LANES, LANES), 1) // HEAD_DIM
        head_ones = jnp.where(li == lj, 1.0, 0.0).astype(BF16)

        def norm_rot(t_ref, gain):
            t = t_ref[...].astype(F32)
            hi, lo = _split_bf16(t * t)
            ss = (jnp.dot(hi, head_ones, preferred_element_type=F32)
                  + jnp.dot(lo, head_ones, preferred_element_type=F32))
            tn = t * lax.rsqrt(ss * (1.0 / HEAD_DIM) + EPS) * gain
            return (tn * cos_ref[...] + pltpu.roll(tn, LANES - ROT_DIM // 2, 1) * sa_ref[...]
                    + pltpu.roll(tn, ROT_DIM // 2, 1) * sb_ref[...])

        qr = norm_rot(q_ref, qg_ref[...]) * (HEAD_DIM ** -0.5 * math.log2(math.e))
        kr = norm_rot(k_ref, kg_ref[...])
        lane = lax.broadcasted_iota(jnp.int32, (s_len, LANES), 1)
        kblk = lax.broadcasted_iota(jnp.int32, (s_len, LANES), 0) // blk
        kaug[0] = jnp.where(lane < HEAD_DIM, kr,
                            jnp.where(lane - HEAD_DIM == kblk, 1.0, 0.0)).astype(BF16)
        kaug[1] = jnp.where(lane >= HEAD_DIM, kr,
                            jnp.where(lane == kblk, 1.0, 0.0)).astype(BF16)
        v = v_ref[...]
        vaug[0] = jnp.where(lane < HEAD_DIM, v, jnp.ones_like(v))
        vaug[1] = jnp.where(lane >= HEAD_DIM, v, jnp.ones_like(v))
        kmean = jnp.sum(kr.reshape(nb, blk, LANES), axis=1) * (1.0 / blk)
        ml = lax.broadcasted_iota(jnp.int32, (nb, LANES), 1)
        km0 = jnp.where(ml < HEAD_DIM, kmean, 0.0)
        km1 = jnp.where(ml >= HEAD_DIM, kmean, 0.0)
        pad = jnp.zeros((HEAD_DIM - nb, LANES), F32)
        wg = jnp.concatenate([km1, pad, km0, pad], axis=0)
        w_hi, w_lo = _split_bf16(wg)
        q_hi, q_lo = _split_bf16(qr)
        gate_t = _dot_nt(w_hi, q_hi) + _dot_nt(w_lo, q_hi) + _dot_nt(w_hi, q_lo)
        n = lax.broadcasted_iota(jnp.int32, (nb, s_len), 0)
        qblk = lax.broadcasted_iota(jnp.int32, (nb, s_len), 1) // blk

        def select_bias_t(g_rows):
            g = jnp.where(n < qblk, g_rows, -jnp.inf)
            bias = jnp.where(n == qblk, 0.0, NEG)
            for k in range(MOBA_TOPK):
                m = jnp.max(g, axis=0, keepdims=True)
                idx = jnp.min(jnp.where(g == m, n, 1 << 30), axis=0, keepdims=True)
                hit = n == idx
                bias = jnp.where(hit & (k < qblk), 0.0, bias)
                g = jnp.where(hit, -jnp.inf, g)
            return bias

        pad_t = jnp.zeros((HEAD_DIM - nb, s_len), F32)
        bias_t = jnp.concatenate([select_bias_t(gate_t[0:nb]), pad_t,
                                  select_bias_t(gate_t[HEAD_DIM:HEAD_DIM + nb]), pad_t], axis=0)
        bias = bias_t.T
        qa_scr[0] = jnp.where(lane < HEAD_DIM, qr, bias).astype(BF16)
        qa_scr[1] = jnp.where(lane >= HEAD_DIM, qr, bias).astype(BF16)

    _prep()
    lane_q = lax.broadcasted_iota(jnp.int32, (blk, LANES), 1)

    def query_block(nblk):
        width = nblk * blk
        row0 = (nblk - 1) * blk
        tri = (lax.broadcasted_iota(jnp.int32, (blk, blk), 1)
               <= lax.broadcasted_iota(jnp.int32, (blk, blk), 0))
        out = []
        for h in range(HEADS_PER_TILE):
            s = _dot_nt(qa_scr[h, row0:row0 + blk, :], kaug[h, 0:width, :])
            own = jnp.where(tri, s[:, width - blk:], NEG)
            s = own if nblk == 1 else jnp.concatenate([s[:, :width - blk], own], axis=1)
            mx = s[:, 0:LANES]
            for t in range(1, width // LANES):
                mx = jnp.maximum(mx, s[:, t * LANES:(t + 1) * LANES])
            m = jnp.broadcast_to(jnp.max(mx, axis=1, keepdims=True), (blk, LANES))
            p = jnp.exp2(s - jnp.concatenate([m] * (width // LANES), axis=1)).astype(BF16)
            out.append(jnp.dot(p, vaug[h, 0:width, :], preferred_element_type=F32))
        a0, a1 = out
        o_ref[row0:row0 + blk, :] = jnp.where(lane_q < HEAD_DIM, a0 / pltpu.roll(a0, HEAD_DIM, 1),
                                              a1 / pltpu.roll(a1, HEAD_DIM, 1)).astype(o_ref.dtype)

    def block_pair(lo):
        query_block(lo)
        if nb + 1 - lo != lo:
            query_block(nb + 1 - lo)

    def one_pair(i, carry):
        for lo in range(1, (nb + 1) // 2 + 1):
            pl.when(i == lo - 1)(functools.partial(block_pair, lo))
        return carry

    lax.fori_loop(0, (nb + 1) // 2, one_pair, 0)


def _attention(p3, cos_t, sa_t, sb_t, qg, kg):
    b, s, _ = p3.shape
    n_tiles = N_HEADS // HEADS_PER_TILE
    qcol, kcol, vcol = 12, 16, 20
    seq = lambda c0: pl.BlockSpec((None, s, LANES), lambda bi, hp: (bi, 0, c0 + hp))
    tab = pl.BlockSpec((s, LANES), lambda bi, hp: (0, 0))
    gain = pl.BlockSpec((1, LANES), lambda bi, hp: (0, 0))
    return pl.pallas_call(
        _attn_kernel,
        grid=(b, n_tiles),
        in_specs=[seq(qcol), seq(kcol), seq(vcol), tab, tab, tab, gain, gain],
        out_specs=pl.BlockSpec((None, s, LANES), lambda bi, hp: (bi, 0, hp)),
        out_shape=jax.ShapeDtypeStruct((b, s, N_HEADS * HEAD_DIM), BF16),
        scratch_shapes=[pltpu.VMEM((HEADS_PER_TILE, s, LANES), BF16),
                        pltpu.VMEM((HEADS_PER_TILE, s, LANES), BF16),
                        pltpu.VMEM((HEADS_PER_TILE, s, LANES), BF16)],
        compiler_params=_cparams(("parallel", "parallel")),
        name="moba_attention",
    )(p3, p3, p3, cos_t, sa_t, sb_t, qg, kg)


def _rotary_tables(s):
    half = ROT_DIM // 2
    inv_freq = ROPE_THETA ** (-jnp.arange(0, ROT_DIM, 2, dtype=F32) / ROT_DIM)
    ang = jnp.arange(s, dtype=jnp.int32).astype(F32)[:, None] * inv_freq[None, :]
    cos, sin = jnp.cos(ang), jnp.sin(ang)
    ones = jnp.ones((s, HEAD_DIM - ROT_DIM), F32)
    zeros = jnp.zeros((s, HEAD_DIM - ROT_DIM), F32)
    zh = jnp.zeros((s, half), F32)
    cos_h = jnp.concatenate([cos, cos, ones], axis=1)
    sa_h = jnp.concatenate([-sin, zh, zeros], axis=1)
    sb_h = jnp.concatenate([zh, sin, zeros], axis=1)
    tile = lambda t: jnp.tile(t, (1, HEADS_PER_TILE))
    return tile(cos_h), tile(sa_h), tile(sb_h)


def _route(logits, n_experts):
    lane = lax.broadcasted_iota(jnp.int32, logits.shape, 1)
    g = jnp.where(lane < n_experts, logits, -jnp.inf)
    m1 = jnp.max(g, axis=1, keepdims=True)
    i1 = jnp.min(jnp.where(g == m1, lane, 1 << 30), axis=1, keepdims=True)
    g2 = jnp.where(lane == i1, -jnp.inf, g)
    m2 = jnp.max(g2, axis=1, keepdims=True)
    i2 = jnp.min(jnp.where(g2 == m2, lane, 1 << 30), axis=1, keepdims=True)
    e2 = jnp.exp(m2 - m1)
    w1 = 1.0 / (1.0 + e2)
    w2 = e2 / (1.0 + e2)
    return jnp.where(lane == 0, w1, jnp.where(lane == 1, w2, jnp.where(
        lane == 2, i1.astype(F32), jnp.where(lane == 3, i2.astype(F32), 0.0))))


def _merge_kernel(*refs, with_router, n_experts):
    if with_router:
        (x_ref, ya_ref, yb_ref, yc_ref, ga_ref, gb_ref, gcg_ref, woa_ref, wob_ref, woc_ref, wo_ref,
         gn_ref, rhi_ref, rlo_ref, x1_ref, h2_ref, route_ref) = refs
    else:
        (x_ref, ya_ref, yb_ref, yc_ref, ga_ref, gb_ref, gcg_ref, woa_ref, wob_ref, woc_ref, wo_ref,
         gn_ref, x1_ref, h2_ref) = refs
    merged = (jax.nn.sigmoid(ga_ref[...].astype(F32))
              * jnp.dot(ya_ref[...], woa_ref[...], preferred_element_type=F32)
              + jax.nn.sigmoid(gb_ref[...].astype(F32))
              * jnp.dot(yb_ref[...], wob_ref[...], preferred_element_type=F32)
              + jax.nn.sigmoid(gcg_ref[...].astype(F32))
              * jnp.dot(yc_ref[...], woc_ref[...], preferred_element_type=F32))
    x1 = x_ref[...] + jnp.dot(merged.astype(BF16), wo_ref[...], preferred_element_type=F32)
    x1_ref[...] = x1
    h2 = _rms(x1, gn_ref[...])
    if with_router:
        n_lane_tiles = h2.shape[1] // LANES
        for j in range(n_lane_tiles):
            h2_ref[pl.ds(j, h2.shape[0], stride=n_lane_tiles), :] = h2[:, j * LANES:(j + 1) * LANES]
        h_hi, h_lo = _split_bf16(h2)
        logits = (jnp.dot(h_hi, rhi_ref[...], preferred_element_type=F32)
                  + jnp.dot(h_hi, rlo_ref[...], preferred_element_type=F32)
                  + jnp.dot(h_lo, rhi_ref[...], preferred_element_type=F32))
        route_ref[...] = _route(logits, n_experts)
    else:
        h2_ref[...] = h2.astype(BF16)


def _merge(x2, ya, yb, yc, p2, woa, wob, woc, wo, gn, router=None, tm=512):
    t, d = x2.shape
    cw = ya.shape[1]
    row = lambda w: pl.BlockSpec((tm, w), lambda i: (i, 0))
    full = lambda arr: pl.BlockSpec(arr.shape, lambda i: (0,) * arr.ndim)
    gate_col0 = (p2.shape[1] - N_BRANCHES * d) // d
    gspec = lambda c: pl.BlockSpec((tm, d), lambda i: (i, gate_col0 + c))
    in_specs = [row(d), row(cw), row(cw), row(cw), gspec(0), gspec(1), gspec(2),
                full(woa), full(wob), full(woc), full(wo), full(gn)]
    args = [x2, ya, yb, yc, p2, p2, p2, woa, wob, woc, wo, gn]
    out_specs = [row(d), row(d)]
    out_shape = [jax.ShapeDtypeStruct((t, d), F32), jax.ShapeDtypeStruct((t, d), BF16)]
    n_experts = 0
    if router is not None:
        n_experts = router.shape[1]
        rpad = jnp.zeros((d, LANES), F32).at[:, :n_experts].set(router)
        r_hi, r_lo = _split_bf16(rpad)
        in_specs += [full(r_hi), full(r_lo)]
        args += [r_hi, r_lo]
        out_specs[1] = pl.BlockSpec((tm * (d // LANES), LANES), lambda i: (i, 0))
        out_shape[1] = jax.ShapeDtypeStruct((t * (d // LANES), LANES), F32)
        out_specs.append(row(LANES))
        out_shape.append(jax.ShapeDtypeStruct((t, LANES), F32))
    return pl.pallas_call(
        functools.partial(_merge_kernel, with_router=router is not None, n_experts=n_experts),
        grid=(t // tm,),
        in_specs=in_specs, out_specs=out_specs, out_shape=out_shape,
        compiler_params=_cparams(("parallel",)),
        name="merge",
    )(*args)


def _ffn_kernel(h_ref, x_ref, wg_ref, wu_ref, wd_ref, o_ref, *, tf):
    h = h_ref[...]
    out = x_ref[...]
    for c in range(wg_ref.shape[1] // tf):
        cols = slice(c * tf, (c + 1) * tf)
        g = jnp.dot(h, wg_ref[:, cols], preferred_element_type=F32)
        u = jnp.dot(h, wu_ref[:, cols], preferred_element_type=F32)
        a = (g * jax.nn.sigmoid(g) * u).astype(BF16)
        out = out + jnp.dot(a, wd_ref[cols, :], preferred_element_type=F32)
    o_ref[...] = out


def _ffn(h2, x1, wg, wu, wd, tm=512, tf=1408):
    t, d = x1.shape
    ff = wg.shape[1]
    resident = lambda arr: pl.BlockSpec(arr.shape, lambda i: (0, 0), pipeline_mode=pl.Buffered(1))
    return pl.pallas_call(
        functools.partial(_ffn_kernel, tf=tf),
        grid=(t // tm,),
        in_specs=[pl.BlockSpec((tm, d), lambda i: (i, 0)),
                  pl.BlockSpec((tm, d), lambda i: (i, 0)),
                  resident(wg), resident(wu), resident(wd)],
        out_specs=pl.BlockSpec((tm, d), lambda i: (i, 0)),
        out_shape=jax.ShapeDtypeStruct((t, d), F32),
        compiler_params=_cparams(("parallel",)),
        name="dense_ffn",
    )(h2, x1, wg, wu, wd)


def _moe_plan(route, n_experts, tm):
    t = route.shape[0]
    ids = route[:, 2:2 + TOP_K].astype(jnp.int32).reshape(-1)
    onehot = (ids[:, None] == jnp.arange(n_experts, dtype=jnp.int32)[None, :]).astype(jnp.int32)
    csum = jnp.cumsum(onehot, axis=0)
    rank = jnp.sum((csum - onehot) * onehot, axis=1)
    counts = csum[-1]
    gsize = (counts + tm - 1) // tm * tm
    gend = jnp.cumsum(gsize)
    gstart = gend - gsize
    dest = gstart[ids] + rank
    n_tiles = (TOP_K * t) // tm + n_experts
    n_rows = n_tiles * tm
    pair = jnp.full((n_rows,), -1, jnp.int32).at[dest].set(jnp.arange(TOP_K * t, dtype=jnp.int32))
    row = jnp.arange(n_rows, dtype=jnp.int32)
    tok = jnp.where(pair >= 0, pair // TOP_K, 0)
    ydst = jnp.where(pair >= 0, (pair % TOP_K) * t + pair // TOP_K, TOP_K * t + row % tm)
    tile_start = jnp.arange(n_tiles, dtype=jnp.int32) * tm
    tile_valid = (tile_start < gend[-1]).astype(jnp.int32)
    tile_expert = jnp.minimum(jnp.sum((tile_start[:, None] >= gend[None, :]).astype(jnp.int32), axis=1),
                              n_experts - 1)
    last_expert = tile_expert[jnp.maximum(jnp.sum(tile_valid) - 1, 0)]
    tile_expert = jnp.where(tile_valid == 1, tile_expert, last_expert)
    spare = TOP_K * t + jnp.arange(tm, dtype=jnp.int32)
    ydst = jnp.concatenate([spare[None, :], ydst.reshape(n_tiles, tm)], axis=0)
    return tok.reshape(n_tiles, tm), ydst, tile_expert, tile_valid


def _moe_ffn_kernel(te_ref, tv_ref, tok_hbm, ydst_hbm, h_hbm, wg_ref, wu_ref, wd_ref, y_hbm,
                    x_buf, x_bf, acc, y_buf, tok_smem, ydst_smem, sem_idx, sem_gather, sem_scatter,
                    *, n_tiles, nf):
    r = pl.program_id(0)
    f = pl.program_id(1)
    tm = acc.shape[0]
    n_lane_tiles = acc.shape[1] // LANES
    share = tm // nf
    valid = tv_ref[r] == 1

    def idx_copies(tile):
        src_tile = jnp.minimum(tile, n_tiles - 1)
        dst = pl.ds(pl.multiple_of((tile & 3) * tm, tm), tm)
        return (pltpu.make_async_copy(tok_hbm.at[src_tile], tok_smem.at[dst], sem_idx.at[0]),
                pltpu.make_async_copy(ydst_hbm.at[src_tile], ydst_smem.at[dst], sem_idx.at[1]))

    def row_tile(buf, k):
        return buf.at[pl.ds(pl.multiple_of(k * n_lane_tiles, n_lane_tiles), n_lane_tiles), :]

    def gather_wait():
        pltpu.make_async_copy(x_buf, x_buf, sem_gather).wait()

    def scatter_wait():
        pltpu.make_async_copy(y_buf, y_buf, sem_scatter).wait()

    def start_gather(slot, k):
        pltpu.make_async_copy(h_hbm.at[tok_smem[slot * tm + k]], row_tile(x_buf, k), sem_gather).start()

    def start_scatter(slot, k):
        pltpu.make_async_copy(row_tile(y_buf, k), y_hbm.at[ydst_smem[slot * tm + k]], sem_scatter).start()

    def issue_share():
        nxt = (r + 1) & 3
        cur = r & 3
        for j in range(share):
            k = f * share + j
            start_gather(nxt, k)
            start_scatter(cur, k)

    @pl.when((r == 0) & (f == 0))
    def _first():
        acc[...] = jnp.zeros(acc.shape, F32)
        for c in idx_copies(0):
            c.start()
        for c in idx_copies(0):
            c.wait()

        def issue(k, carry):
            start_gather(0, k)
            return carry

        lax.fori_loop(0, tm, issue, 0, unroll=8)
        for c in idx_copies(1):
            c.start()

    @pl.when(f == 0)
    def _tile_start():
        @pl.when(r >= 1)
        def _():
            scatter_wait()

        for j in range(n_lane_tiles):
            y_buf[pl.ds(j, tm, stride=n_lane_tiles), :] = acc[:, j * LANES:(j + 1) * LANES]
        gather_wait()
        for j in range(n_lane_tiles):
            x_bf[:, j * LANES:(j + 1) * LANES] = x_buf[pl.ds(j, tm, stride=n_lane_tiles), :].astype(BF16)
        for c in idx_copies(r + 1):
            c.wait()
        for c in idx_copies(r + 2):
            c.start()

    @pl.when(valid)
    def _compute():
        issue_share()
        x = x_bf[...]
        g = jnp.dot(x, wg_ref[...], preferred_element_type=F32)
        u = jnp.dot(x, wu_ref[...], preferred_element_type=F32)
        a = (g * jax.nn.sigmoid(g) * u).astype(BF16)
        part = jnp.dot(a, wd_ref[...], preferred_element_type=F32)

        @pl.when(f == 0)
        def _():
            acc[...] = part

        @pl.when(f > 0)
        def _():
            acc[...] += part

    @pl.when(jnp.logical_not(valid))
    def _idle():
        issue_share()

    @pl.when((r == n_tiles - 1) & (f == nf - 1))
    def _drain():
        gather_wait()
        scatter_wait()
        for c in idx_copies(r + 2):
            c.wait()


def _moe_ffn(h2, tok, ydst, tile_expert, tile_valid, wg, wu, wd, tf=896):
    d = wg.shape[1]
    n_lane_tiles = d // LANES
    t = h2.shape[0] // n_lane_tiles
    h2 = h2.reshape(t, n_lane_tiles, LANES)
    n_tiles, tm = tok.shape
    ff = wg.shape[2]
    nf = ff // tf
    assert tm % nf == 0 and ydst.shape == (n_tiles + 1, tm)
    fsel = lambda f, tv, r: jnp.where(tv[r] == 1, f, nf - 1)
    grid_spec = pltpu.PrefetchScalarGridSpec(
        num_scalar_prefetch=2,
        grid=(n_tiles, nf),
        in_specs=[pl.BlockSpec(memory_space=pl.ANY), pl.BlockSpec(memory_space=pl.ANY),
                  pl.BlockSpec(memory_space=pl.ANY),
                  pl.BlockSpec((None, d, tf), lambda r, f, te, tv: (te[r], 0, fsel(f, tv, r))),
                  pl.BlockSpec((None, d, tf), lambda r, f, te, tv: (te[r], 0, fsel(f, tv, r))),
                  pl.BlockSpec((None, tf, d), lambda r, f, te, tv: (te[r], fsel(f, tv, r), 0))],
        out_specs=pl.BlockSpec(memory_space=pl.ANY),
        scratch_shapes=[pltpu.VMEM((tm * n_lane_tiles, LANES), F32), pltpu.VMEM((tm, d), BF16),
                        pltpu.VMEM((tm, d), F32), pltpu.VMEM((tm * n_lane_tiles, LANES), F32),
                        pltpu.SMEM((4 * tm,), jnp.int32), pltpu.SMEM((4 * tm,), jnp.int32),
                        pltpu.SemaphoreType.DMA((2,)), pltpu.SemaphoreType.DMA, pltpu.SemaphoreType.DMA],
    )
    return pl.pallas_call(
        functools.partial(_moe_ffn_kernel, n_tiles=n_tiles, nf=nf),
        grid_spec=grid_spec,
        out_shape=jax.ShapeDtypeStruct((TOP_K * t + tm, n_lane_tiles, LANES), F32),
        compiler_params=_cparams(("arbitrary", "arbitrary")),
        name="moe_ffn",
    )(tile_expert, tile_valid, tok, ydst, h2, wg, wu, wd)


def _combine_kernel(x_ref, y0_ref, y1_ref, route_ref, o_ref):
    route = route_ref[...]
    lane = lax.broadcasted_iota(jnp.int32, route.shape, 1)
    w0 = jnp.sum(jnp.where(lane == 0, route, 0.0), axis=1, keepdims=True)
    w1 = jnp.sum(jnp.where(lane == 1, route, 0.0), axis=1, keepdims=True)
    tm, d = x_ref.shape
    n_lane_tiles = d // LANES
    for j in range(n_lane_tiles):
        cols = slice(j * LANES, (j + 1) * LANES)
        rows = pl.ds(j, tm, stride=n_lane_tiles)
        o_ref[:, cols] = x_ref[:, cols] + w0 * y0_ref[rows, :] + w1 * y1_ref[rows, :]


def _combine(x1, y, route, tm=512):
    t, d = x1.shape
    nblk = t // tm
    return pl.pallas_call(
        _combine_kernel,
        grid=(nblk,),
        in_specs=[pl.BlockSpec((tm, d), lambda i: (i, 0)),
                  pl.BlockSpec((tm * (d // LANES), LANES), lambda i: (i, 0)),
                  pl.BlockSpec((tm * (d // LANES), LANES), lambda i: (i + nblk, 0)),
                  pl.BlockSpec((tm, LANES), lambda i: (i, 0))],
        out_specs=pl.BlockSpec((tm, d), lambda i: (i, 0)),
        out_shape=jax.ShapeDtypeStruct((t, d), F32),
        compiler_params=_cparams(("parallel",)),
        name="moe_combine",
    )(x1, y, y, route)


def _moe(h2, x1, route, wg, wu, wd, tm=1024):
    tok, ydst, tile_expert, tile_valid = _moe_plan(route, wg.shape[0], tm)
    y = _moe_ffn(h2, tok, ydst, tile_expert, tile_valid, wg, wu, wd)
    return _combine(x1, y.reshape(-1, LANES), route)


def _block_diag(w):
    nb, n, _ = w.shape
    eye = jnp.eye(nb, dtype=w.dtype)
    return (eye[:, None, :, None] * w[:, :, None, :]).reshape(nb * n, nb * n)


def kernel(x, norm_mix, w_in, conv_a_w, wo_a, q_norm, k_norm, wo_b, conv_c_w, conv_c_b, rg_w_r, rg_b_r, rg_w_i, rg_b_i, rg_lambda, wo_c, w_o, norm_ffn, ffn_w_gate, ffn_w_up, ffn_w_down, moe_router, moe_w_gate, moe_w_up, moe_w_down):
    bsz, s, d = x.shape
    depth = w_in.shape[0]
    cos_t, sa_t, sb_t = _rotary_tables(s)
    x2 = x.reshape(bsz * s, d)
    for l in range(depth):
        p2 = _inproj(x2, norm_mix[l][None, :], w_in[l].astype(BF16))
        p3 = p2.reshape(bsz, s, p2.shape[1])
        ya, yc = _seqmix(p3, conv_a_w[l], conv_c_w[l], conv_c_b[l][None, :],
                         _block_diag(rg_w_r[l]).astype(BF16), rg_b_r[l][None, :],
                         _block_diag(rg_w_i[l]).astype(BF16), rg_b_i[l][None, :],
                         rg_lambda[l][None, :])
        yb = _attention(p3, cos_t, sa_t, sb_t,
                        jnp.tile(q_norm[l], HEADS_PER_TILE)[None, :],
                        jnp.tile(k_norm[l], HEADS_PER_TILE)[None, :])
        cw = ya.shape[2]
        args = (x2, ya.reshape(bsz * s, cw), yb.reshape(bsz * s, yb.shape[2]), yc.reshape(bsz * s, cw), p2,
                wo_a[l].astype(BF16), wo_b[l].astype(BF16), wo_c[l].astype(BF16), w_o[l].astype(BF16),
                norm_ffn[l][None, :])
        if l % 2 == 0:
            x1, h2 = _merge(*args)
            x2 = _ffn(h2, x1, ffn_w_gate[l // 2].astype(BF16), ffn_w_up[l // 2].astype(BF16),
                      ffn_w_down[l // 2].astype(BF16))
        else:
            x1, h2, route = _merge(*args, router=moe_router[l // 2])
            x2 = _moe(h2, x1, route, moe_w_gate[l // 2].astype(BF16), moe_w_up[l // 2].astype(BF16),
                      moe_w_down[l // 2].astype(BF16))
    return x2.reshape(bsz, s, d)
```

```python
import functools
import math

import jax
import jax.numpy as jnp
from jax import lax
from jax.experimental import pallas as pl
from jax.experimental.pallas import tpu as pltpu

F32 = jnp.float32
BF16 = jnp.bfloat16

N_HEADS = 8
HEAD_DIM = 64
ROT_DIM = HEAD_DIM // 4
ROPE_THETA = 500000.0
MOBA_BLOCK = 256
MOBA_TOPK = 3
LRU_C = 8.0
N_BRANCHES = 3
TOP_K = 2
EPS = 1e-6
NEG = -1e30

LANES = 128
SUBLANES = 8
HEADS_PER_TILE = LANES // HEAD_DIM
VMEM_LIMIT = 56 * 1024 * 1024


def _cparams(sem):
    return pltpu.CompilerParams(dimension_semantics=sem, vmem_limit_bytes=VMEM_LIMIT)


def _split_bf16(x):
    hi = x.astype(BF16)
    lo = (x - hi.astype(F32)).astype(BF16)
    return hi, lo


def _rms(x, g):
    ms = jnp.mean(x * x, axis=-1, keepdims=True)
    return x * lax.rsqrt(ms + EPS) * g


def _inproj_kernel(x_ref, g_ref, w_ref, o_ref, *, tn):
    h = _rms(x_ref[...], g_ref[...]).astype(BF16)
    for c in range(o_ref.shape[1] // tn):
        cols = slice(c * tn, (c + 1) * tn)
        o_ref[:, cols] = jnp.dot(h, w_ref[:, cols], preferred_element_type=F32).astype(o_ref.dtype)


def _inproj(x2, g, w, tm=512, tn=1024):
    t, d = x2.shape
    n = w.shape[1]
    return pl.pallas_call(
        functools.partial(_inproj_kernel, tn=tn),
        grid=(t // tm,),
        in_specs=[pl.BlockSpec((tm, d), lambda i: (i, 0)),
                  pl.BlockSpec((1, d), lambda i: (0, 0)),
                  pl.BlockSpec((d, n), lambda i: (0, 0), pipeline_mode=pl.Buffered(1))],
        out_specs=pl.BlockSpec((tm, n), lambda i: (i, 0)),
        out_shape=jax.ShapeDtypeStruct((t, n), BF16),
        compiler_params=_cparams(("parallel",)),
        name="inproj",
    )(x2, g, w)


HALO = 8


def _seqmix_kernel(xa_ref, ba_ref, ca_ref, xc_ref, gc_ref, wa_ref, wc_ref, bc_ref,
                   wr_ref, br_ref, wi_ref, bi_ref, lam_ref, ya_ref, yc_ref,
                   ua_buf, xc_buf, h_carry):
    ts = xa_ref.shape[0]

    @pl.when(pl.program_id(1) == 0)
    def _():
        ua_buf[...] = jnp.zeros(ua_buf.shape, F32)
        xc_buf[...] = jnp.zeros(xc_buf.shape, F32)
        h_carry[...] = jnp.zeros_like(h_carry)

    def causal_conv(u, w_ref, tail_ref):
        k_w = w_ref.shape[0]
        row = lax.broadcasted_iota(jnp.int32, u.shape, 0)
        prev = jnp.concatenate([tail_ref[...]] * (ts // HALO), axis=0)
        y = w_ref[k_w - 1:k_w, :] * u
        for d in range(1, k_w):
            shifted = jnp.where(row >= d, pltpu.roll(u, d, 0), pltpu.roll(prev, d, 0))
            y = y + w_ref[k_w - 1 - d:k_w - d, :] * shifted
        tail_ref[...] = u[ts - HALO:ts, :]
        return y

    conv_a = causal_conv(ca_ref[...].astype(F32) * xa_ref[...].astype(F32), wa_ref, ua_buf)
    ya_ref[...] = (ba_ref[...].astype(F32) * conv_a).astype(ya_ref.dtype)

    xcv = causal_conv(xc_ref[...].astype(F32), wc_ref, xc_buf) + bc_ref[...]


    xb = xcv.astype(BF16)
    r = jax.nn.sigmoid(jnp.dot(xb, wr_ref[...], preferred_element_type=F32) + br_ref[...])
    ig = jax.nn.sigmoid(jnp.dot(xb, wi_ref[...], preferred_element_type=F32) + bi_ref[...])
    lam = lam_ref[...]
    softplus_neg_lam = jnp.maximum(-lam, 0.0) + jnp.log1p(jnp.exp(-jnp.abs(lam)))
    log_a = (-LRU_C) * r * softplus_neg_lam
    a = jnp.exp(log_a)
    u = jnp.sqrt(-jnp.tanh(log_a) * (1.0 + a * a)) * (ig * xcv)

    n_groups = ts // SUBLANES
    a = a.reshape(n_groups, SUBLANES, a.shape[1])
    u = u.reshape(n_groups, SUBLANES, u.shape[1])
    sub = lax.broadcasted_iota(jnp.int32, a.shape, 1)
    d = 1
    while d < SUBLANES:
        a_sh = jnp.where(sub >= d, pltpu.roll(a, d, 1), 1.0)
        u_sh = jnp.where(sub >= d, pltpu.roll(u, d, 1), 0.0)
        u = a * u_sh + u
        a = a * a_sh
        d *= 2
    carry = h_carry[...]
    groups = []
    for g in range(n_groups):
        hg = a[g] * carry + u[g]
        carry = hg[SUBLANES - 1:SUBLANES, :]
        groups.append(hg)
    h = jnp.concatenate(groups, axis=0)
    h_carry[...] = carry

    gc = gc_ref[...].astype(F32)
    gelu = 0.5 * gc * (1.0 + jnp.tanh(math.sqrt(2.0 / math.pi) * (gc + 0.044715 * (gc * gc * gc))))
    yc_ref[...] = (gelu * h).astype(yc_ref.dtype)


def _seqmix(p3, wa, wc, bc, wr_bd, br, wi_bd, bi, lam, ts=256):
    b, s, _ = p3.shape
    cw = wa.shape[1]
    col = lambda c: pl.BlockSpec((None, ts, cw), lambda bi_, si: (bi_, si, c))
    full = lambda arr: pl.BlockSpec(arr.shape, lambda bi_, si: (0,) * arr.ndim)
    out = jax.ShapeDtypeStruct((b, s, cw), BF16)
    ospec = pl.BlockSpec((None, ts, cw), lambda bi_, si: (bi_, si, 0))
    return pl.pallas_call(
        _seqmix_kernel,
        grid=(b, s // ts),
        in_specs=[col(0), col(1), col(2), col(6), col(7),
                  full(wa), full(wc), full(bc), full(wr_bd), full(br), full(wi_bd), full(bi), full(lam)],
        out_specs=[ospec, ospec],
        out_shape=[out, out],
        scratch_shapes=[pltpu.VMEM((HALO, cw), F32), pltpu.VMEM((HALO, cw), F32),
                        pltpu.VMEM((1, cw), F32)],
        compiler_params=_cparams(("parallel", "arbitrary")),
        name="seqmix",
    )(p3, p3, p3, p3, p3, wa, wc, bc, wr_bd, br, wi_bd, bi, lam)


def _dot_nt(a, b):
    return lax.dot_general(a, b, (((1,), (1,)), ((), ())), preferred_element_type=F32)


def _attn_kernel(q_ref, k_ref, v_ref, cos_ref, sa_ref, sb_ref, qg_ref, kg_ref, o_ref,
                 qa_scr, kaug, vaug):
    s_len = k_ref.shape[0]
    blk = MOBA_BLOCK
    nb = s_len // blk

    def _prep():
        li = lax.broadcasted_iota(jnp.int32, (LANES, LANES), 0) // HEAD_DIM
        lj = lax.broadcasted_iota(jnp.int32, (LANES, LANES), 1) // HEAD_DIM
        head_ones = jnp.where(li == lj, 1.0, 0.0).astype(BF16)

        def norm_rot(t_ref, gain):
            t = t_ref[...].astype(F32)
            hi, lo = _split_bf16(t * t)
            ss = (jnp.dot(hi, head_ones, preferred_element_type=F32)
                  + jnp.dot(lo, head_ones, preferred_element_type=F32))
            tn = t * lax.rsqrt(ss * (1.0 / HEAD_DIM) + EPS) * gain
            return (tn * cos_ref[...] + pltpu.roll(tn, LANES - ROT_DIM // 2, 1) * sa_ref[...]
                    + pltpu.roll(tn, ROT_DIM // 2, 1) * sb_ref[...])

        qr = norm_rot(q_ref, qg_ref[...]) * (HEAD_DIM ** -0.5 * math.log2(math.e))
        kr = norm_rot(k_ref, kg_ref[...])
        lane = lax.broadcasted_iota(jnp.int32, (s_len, LANES), 1)
        kblk = lax.broadcasted_iota(jnp.int32, (s_len, LANES), 0) // blk
        kaug[0] = jnp.where(lane < HEAD_DIM, kr,
                            jnp.where(lane - HEAD_DIM == kblk, 1.0, 0.0)).astype(BF16)
        kaug[1] = jnp.where(lane >= HEAD_DIM, kr,
                            jnp.where(lane == kblk, 1.0, 0.0)).astype(BF16)
        v = v_ref[...]
        vaug[0] = jnp.where(lane < HEAD_DIM, v, jnp.ones_like(v))
        vaug[1] = jnp.where(lane >= HEAD_DIM, v, jnp.ones_like(v))
        kmean = jnp.sum(kr.reshape(nb, blk, LANES), axis=1) * (1.0 / blk)
        ml = lax.broadcasted_iota(jnp.int32, (nb, LANES), 1)
        km0 = jnp.where(ml < HEAD_DIM, kmean, 0.0)
        km1 = jnp.where(ml >= HEAD_DIM, kmean, 0.0)
        pad = jnp.zeros((HEAD_DIM - nb, LANES), F32)
        wg = jnp.concatenate([km1, pad, km0, pad], axis=0)
        w_hi, w_lo = _split_bf16(wg)
        q_hi, q_lo = _split_bf16(qr)
        gate_t = _dot_nt(w_hi, q_hi) + _dot_nt(w_lo, q_hi) + _dot_nt(w_hi, q_lo)
        n = lax.broadcasted_iota(jnp.int32, (nb, s_len), 0)
        qblk = lax.broadcasted_iota(jnp.int32, (nb, s_len), 1) // blk

        def select_bias_t(g_rows):
            g = jnp.where(n < qblk, g_rows, -jnp.inf)
            bias = jnp.where(n == qblk, 0.0, NEG)
            for k in range(MOBA_TOPK):
                m = jnp.max(g, axis=0, keepdims=True)
                idx = jnp.min(jnp.where(g == m, n, 1 << 30), axis=0, keepdims=True)
                hit = n == idx
                bias = jnp.where(hit & (k < qblk), 0.0, bias)
                g = jnp.where(hit, -jnp.inf, g)
            return bias

        pad_t = jnp.zeros((HEAD_DIM - nb, s_len), F32)
        bias_t = jnp.concatenate([select_bias_t(gate_t[0:nb]), pad_t,
                                  select_bias_t(gate_t[HEAD_DIM:HEAD_DIM + nb]), pad_t], axis=0)
        bias = bias_t.T
        qa_scr[0] = jnp.where(lane < HEAD_DIM, qr, bias).astype(BF16)
        qa_scr[1] = jnp.where(lane >= HEAD_DIM, qr, bias).astype(BF16)

    _prep()
    lane_q = lax.broadcasted_iota(jnp.int32, (blk, LANES), 1)

    def query_block(nblk):
        width = nblk * blk
        row0 = (nblk - 1) * blk
        tri = (lax.broadcasted_iota(jnp.int32, (blk, blk), 1)
               <= lax.broadcasted_iota(jnp.int32, (blk, blk), 0))
        out = []
        for h in range(HEADS_PER_TILE):
            s = _dot_nt(qa_scr[h, row0:row0 + blk, :], kaug[h, 0:width, :])
            own = jnp.where(tri, s[:, width - blk:], NEG)
            s = own if nblk == 1 else jnp.concatenate([s[:, :width - blk], own], axis=1)
            mx = s[:, 0:LANES]
            for t in range(1, width // LANES):
                mx = jnp.maximum(mx, s[:, t * LANES:(t + 1) * LANES])
            m = jnp.broadcast_to(jnp.max(mx, axis=1, keepdims=True), (blk, LANES))
            p = jnp.exp2(s - jnp.concatenate([m] * (width // LANES), axis=1)).astype(BF16)
            out.append(jnp.dot(p, vaug[h, 0:width, :], preferred_element_type=F32))
        a0, a1 = out
        o_ref[row0:row0 + blk, :] = jnp.where(lane_q < HEAD_DIM, a0 / pltpu.roll(a0, HEAD_DIM, 1),
                                              a1 / pltpu.roll(a1, HEAD_DIM, 1)).astype(o_ref.dtype)

    def block_pair(lo):
        query_block(lo)
        if nb + 1 - lo != lo:
            query_block(nb + 1 - lo)

    n_pairs = (nb + 1) // 2
    n_groups = (n_pairs + 1) // 2

    def pair_group(g):
        block_pair(g + 1)
        if g + 1 + n_groups <= n_pairs:
            block_pair(g + 1 + n_groups)

    def one_group(i, carry):
        for g in range(n_groups):
            pl.when(i == g)(functools.partial(pair_group, g))
        return carry

    lax.fori_loop(0, n_groups, one_group, 0)


def _attention(p3, cos_t, sa_t, sb_t, qg, kg):
    b, s, _ = p3.shape
    n_tiles = N_HEADS // HEADS_PER_TILE
    qcol, kcol, vcol = 12, 16, 20
    seq = lambda c0: pl.BlockSpec((None, s, LANES), lambda bi, hp: (bi, 0, c0 + hp))
    tab = pl.BlockSpec((s, LANES), lambda bi, hp: (0, 0))
    gain = pl.BlockSpec((1, LANES), lambda bi, hp: (0, 0))
    return pl.pallas_call(
        _attn_kernel,
        grid=(b, n_tiles),
        in_specs=[seq(qcol), seq(kcol), seq(vcol), tab, tab, tab, gain, gain],
        out_specs=pl.BlockSpec((None, s, LANES), lambda bi, hp: (bi, 0, hp)),
        out_shape=jax.ShapeDtypeStruct((b, s, N_HEADS * HEAD_DIM), BF16),
        scratch_shapes=[pltpu.VMEM((HEADS_PER_TILE, s, LANES), BF16),
                        pltpu.VMEM((HEADS_PER_TILE, s, LANES), BF16),
                        pltpu.VMEM((HEADS_PER_TILE, s, LANES), BF16)],
        compiler_params=_cparams(("parallel", "parallel")),
        name="moba_attention",
    )(p3, p3, p3, cos_t, sa_t, sb_t, qg, kg)


def _rotary_tables(s):
    half = ROT_DIM // 2
    inv_freq = ROPE_THETA ** (-jnp.arange(0, ROT_DIM, 2, dtype=F32) / ROT_DIM)
    ang = jnp.arange(s, dtype=jnp.int32).astype(F32)[:, None] * inv_freq[None, :]
    cos, sin = jnp.cos(ang), jnp.sin(ang)
    ones = jnp.ones((s, HEAD_DIM - ROT_DIM), F32)
    zeros = jnp.zeros((s, HEAD_DIM - ROT_DIM), F32)
    zh = jnp.zeros((s, half), F32)
    cos_h = jnp.concatenate([cos, cos, ones], axis=1)
    sa_h = jnp.concatenate([-sin, zh, zeros], axis=1)
    sb_h = jnp.concatenate([zh, sin, zeros], axis=1)
    tile = lambda t: jnp.tile(t, (1, HEADS_PER_TILE))
    return tile(cos_h), tile(sa_h), tile(sb_h)


def _route(logits, n_experts):
    lane = lax.broadcasted_iota(jnp.int32, logits.shape, 1)
    g = jnp.where(lane < n_experts, logits, -jnp.inf)
    m1 = jnp.max(g, axis=1, keepdims=True)
    i1 = jnp.min(jnp.where(g == m1, lane, 1 << 30), axis=1, keepdims=True)
    g2 = jnp.where(lane == i1, -jnp.inf, g)
    m2 = jnp.max(g2, axis=1, keepdims=True)
    i2 = jnp.min(jnp.where(g2 == m2, lane, 1 << 30), axis=1, keepdims=True)
    e2 = jnp.exp(m2 - m1)
    w1 = 1.0 / (1.0 + e2)
    w2 = e2 / (1.0 + e2)
    return jnp.where(lane == 0, w1, jnp.where(lane == 1, w2, jnp.where(
        lane == 2, i1.astype(F32), jnp.where(lane == 3, i2.astype(F32), 0.0))))


def _merge_kernel(*refs, with_router, n_experts):
    if with_router:
        (x_ref, ya_ref, yb_ref, yc_ref, ga_ref, gb_ref, gcg_ref, woa_ref, wob_ref, woc_ref, wo_ref,
         gn_ref, rhi_ref, rlo_ref, x1_ref, h2_ref, route_ref) = refs
    else:
        (x_ref, ya_ref, yb_ref, yc_ref, ga_ref, gb_ref, gcg_ref, woa_ref, wob_ref, woc_ref, wo_ref,
         gn_ref, x1_ref, h2_ref) = refs
    merged = (jax.nn.sigmoid(ga_ref[...].astype(F32))
              * jnp.dot(ya_ref[...], woa_ref[...], preferred_element_type=F32)
              + jax.nn.sigmoid(gb_ref[...].astype(F32))
              * jnp.dot(yb_ref[...], wob_ref[...], preferred_element_type=F32)
              + jax.nn.sigmoid(gcg_ref[...].astype(F32))
              * jnp.dot(yc_ref[...], woc_ref[...], preferred_element_type=F32))
    x1 = x_ref[...] + jnp.dot(merged.astype(BF16), wo_ref[...], preferred_element_type=F32)
    x1_ref[...] = x1
    h2 = _rms(x1, gn_ref[...])
    if with_router:
        n_lane_tiles = h2.shape[1] // LANES
        for j in range(n_lane_tiles):
            h2_ref[pl.ds(j, h2.shape[0], stride=n_lane_tiles), :] = h2[:, j * LANES:(j + 1) * LANES]
        h_hi, h_lo = _split_bf16(h2)
        logits = (jnp.dot(h_hi, rhi_ref[...], preferred_element_type=F32)
                  + jnp.dot(h_hi, rlo_ref[...], preferred_element_type=F32)
                  + jnp.dot(h_lo, rhi_ref[...], preferred_element_type=F32))
        route_ref[...] = _route(logits, n_experts)
    else:
        h2_ref[...] = h2.astype(BF16)


def _merge(x2, ya, yb, yc, p2, woa, wob, woc, wo, gn, router=None, tm=512):
    t, d = x2.shape
    cw = ya.shape[1]
    row = lambda w: pl.BlockSpec((tm, w), lambda i: (i, 0))
    full = lambda arr: pl.BlockSpec(arr.shape, lambda i: (0,) * arr.ndim)
    gate_col0 = (p2.shape[1] - N_BRANCHES * d) // d
    gspec = lambda c: pl.BlockSpec((tm, d), lambda i: (i, gate_col0 + c))
    in_specs = [row(d), row(cw), row(cw), row(cw), gspec(0), gspec(1), gspec(2),
                full(woa), full(wob), full(woc), full(wo), full(gn)]
    args = [x2, ya, yb, yc, p2, p2, p2, woa, wob, woc, wo, gn]
    out_specs = [row(d), row(d)]
    out_shape = [jax.ShapeDtypeStruct((t, d), F32), jax.ShapeDtypeStruct((t, d), BF16)]
    n_experts = 0
    if router is not None:
        n_experts = router.shape[1]
        rpad = jnp.zeros((d, LANES), F32).at[:, :n_experts].set(router)
        r_hi, r_lo = _split_bf16(rpad)
        in_specs += [full(r_hi), full(r_lo)]
        args += [r_hi, r_lo]
        out_specs[1] = pl.BlockSpec((tm * (d // LANES), LANES), lambda i: (i, 0))
        out_shape[1] = jax.ShapeDtypeStruct((t * (d // LANES), LANES), F32)
        out_specs.append(row(LANES))
        out_shape.append(jax.ShapeDtypeStruct((t, LANES), F32))
    return pl.pallas_call(
        functools.partial(_merge_kernel, with_router=router is not None, n_experts=n_experts),
        grid=(t // tm,),
        in_specs=in_specs, out_specs=out_specs, out_shape=out_shape,
        compiler_params=_cparams(("parallel",)),
        name="merge",
    )(*args)


def _ffn_kernel(h_ref, x_ref, wg_ref, wu_ref, wd_ref, o_ref, *, tf):
    h = h_ref[...]
    out = x_ref[...]
    for c in range(wg_ref.shape[1] // tf):
        cols = slice(c * tf, (c + 1) * tf)
        g = jnp.dot(h, wg_ref[:, cols], preferred_element_type=F32)
        u = jnp.dot(h, wu_ref[:, cols], preferred_element_type=F32)
        a = (g * jax.nn.sigmoid(g) * u).astype(BF16)
        out = out + jnp.dot(a, wd_ref[cols, :], preferred_element_type=F32)
    o_ref[...] = out


def _ffn(h2, x1, wg, wu, wd, tm=512, tf=1408):
    t, d = x1.shape
    ff = wg.shape[1]
    resident = lambda arr: pl.BlockSpec(arr.shape, lambda i: (0, 0), pipeline_mode=pl.Buffered(1))
    return pl.pallas_call(
        functools.partial(_ffn_kernel, tf=tf),
        grid=(t // tm,),
        in_specs=[pl.BlockSpec((tm, d), lambda i: (i, 0)),
                  pl.BlockSpec((tm, d), lambda i: (i, 0)),
                  resident(wg), resident(wu), resident(wd)],
        out_specs=pl.BlockSpec((tm, d), lambda i: (i, 0)),
        out_shape=jax.ShapeDtypeStruct((t, d), F32),
        compiler_params=_cparams(("parallel",)),
        name="dense_ffn",
    )(h2, x1, wg, wu, wd)


def _moe_plan(route, n_experts, tm):
    t = route.shape[0]
    ids = route[:, 2:2 + TOP_K].astype(jnp.int32).reshape(-1)
    onehot = (ids[:, None] == jnp.arange(n_experts, dtype=jnp.int32)[None, :]).astype(jnp.int32)
    csum = jnp.cumsum(onehot, axis=0)
    rank = jnp.sum((csum - onehot) * onehot, axis=1)
    counts = csum[-1]
    gsize = (counts + tm - 1) // tm * tm
    gend = jnp.cumsum(gsize)
    gstart = gend - gsize
    dest = gstart[ids] + rank
    n_tiles = (TOP_K * t) // tm + n_experts
    n_rows = n_tiles * tm
    pair = jnp.full((n_rows,), -1, jnp.int32).at[dest].set(jnp.arange(TOP_K * t, dtype=jnp.int32))
    row = jnp.arange(n_rows, dtype=jnp.int32)
    tok = jnp.where(pair >= 0, pair // TOP_K, 0)
    ydst = jnp.where(pair >= 0, (pair % TOP_K) * t + pair // TOP_K, TOP_K * t + row % tm)
    tile_start = jnp.arange(n_tiles, dtype=jnp.int32) * tm
    tile_valid = (tile_start < gend[-1]).astype(jnp.int32)
    tile_expert = jnp.minimum(jnp.sum((tile_start[:, None] >= gend[None, :]).astype(jnp.int32), axis=1),
                              n_experts - 1)
    last_expert = tile_expert[jnp.maximum(jnp.sum(tile_valid) - 1, 0)]
    tile_expert = jnp.where(tile_valid == 1, tile_expert, last_expert)
    spare = TOP_K * t + jnp.arange(tm, dtype=jnp.int32)
    ydst = jnp.concatenate([spare[None, :], ydst.reshape(n_tiles, tm)], axis=0)
    return tok.reshape(n_tiles, tm), ydst, tile_expert, tile_valid


def _moe_ffn_kernel(te_ref, tv_ref, tok_hbm, ydst_hbm, h_hbm, wg_ref, wu_ref, wd_ref, y_hbm,
                    x_buf, x_bf, acc, y_buf, tok_smem, ydst_smem, sem_idx, sem_gather, sem_scatter,
                    *, n_tiles, nf):
    r = pl.program_id(0)
    f = pl.program_id(1)
    tm = acc.shape[0]
    n_lane_tiles = acc.shape[1] // LANES
    share = tm // nf
    valid = tv_ref[r] == 1

    def idx_copies(tile):
        src_tile = jnp.minimum(tile, n_tiles - 1)
        dst = pl.ds(pl.multiple_of((tile & 3) * tm, tm), tm)
        return (pltpu.make_async_copy(tok_hbm.at[src_tile], tok_smem.at[dst], sem_idx.at[0]),
                pltpu.make_async_copy(ydst_hbm.at[src_tile], ydst_smem.at[dst], sem_idx.at[1]))

    def row_tile(buf, k):
        return buf.at[pl.ds(pl.multiple_of(k * n_lane_tiles, n_lane_tiles), n_lane_tiles), :]

    def gather_wait():
        pltpu.make_async_copy(x_buf, x_buf, sem_gather).wait()

    def scatter_wait():
        pltpu.make_async_copy(y_buf, y_buf, sem_scatter).wait()

    def start_gather(slot, k):
        pltpu.make_async_copy(h_hbm.at[tok_smem[slot * tm + k]], row_tile(x_buf, k), sem_gather).start()

    def start_scatter(slot, k):
        pltpu.make_async_copy(row_tile(y_buf, k), y_hbm.at[ydst_smem[slot * tm + k]], sem_scatter).start()

    def issue_share():
        nxt = (r + 1) & 3
        cur = r & 3
        for j in range(share):
            k = f * share + j
            start_gather(nxt, k)
            start_scatter(cur, k)

    @pl.when((r == 0) & (f == 0))
    def _first():
        acc[...] = jnp.zeros(acc.shape, F32)
        for c in idx_copies(0):
            c.start()
        for c in idx_copies(0):
            c.wait()

        def issue(k, carry):
            start_gather(0, k)
            return carry

        lax.fori_loop(0, tm, issue, 0, unroll=8)
        for c in idx_copies(1):
            c.start()

    @pl.when(f == 0)
    def _tile_start():
        @pl.when(r >= 1)
        def _():
            scatter_wait()

        for j in range(n_lane_tiles):
            y_buf[pl.ds(j, tm, stride=n_lane_tiles), :] = acc[:, j * LANES:(j + 1) * LANES]
        gather_wait()
        for j in range(n_lane_tiles):
            x_bf[:, j * LANES:(j + 1) * LANES] = x_buf[pl.ds(j, tm, stride=n_lane_tiles), :].astype(BF16)
        for c in idx_copies(r + 1):
            c.wait()
        for c in idx_copies(r + 2):
            c.start()

    @pl.when(valid)
    def _compute():
        issue_share()
        x = x_bf[...]
        g = jnp.dot(x, wg_ref[...], preferred_element_type=F32)
        u = jnp.dot(x, wu_ref[...], preferred_element_type=F32)
        a = (g * jax.nn.sigmoid(g) * u).astype(BF16)
        part = jnp.dot(a, wd_ref[...], preferred_element_type=F32)

        @pl.when(f == 0)
        def _():
            acc[...] = part

        @pl.when(f > 0)
        def _():
            acc[...] += part

    @pl.when(jnp.logical_not(valid))
    def _idle():
        issue_share()

    @pl.when((r == n_tiles - 1) & (f == nf - 1))
    def _drain():
        gather_wait()
        scatter_wait()
        for c in idx_copies(r + 2):
            c.wait()


def _moe_ffn(h2, tok, ydst, tile_expert, tile_valid, wg, wu, wd, tf=896):
    d = wg.shape[1]
    n_lane_tiles = d // LANES
    t = h2.shape[0] // n_lane_tiles
    h2 = h2.reshape(t, n_lane_tiles, LANES)
    n_tiles, tm = tok.shape
    ff = wg.shape[2]
    nf = ff // tf
    assert tm % nf == 0 and ydst.shape == (n_tiles + 1, tm)
    fsel = lambda f, tv, r: jnp.where(tv[r] == 1, f, nf - 1)
    grid_spec = pltpu.PrefetchScalarGridSpec(
        num_scalar_prefetch=2,
        grid=(n_tiles, nf),
        in_specs=[pl.BlockSpec(memory_space=pl.ANY), pl.BlockSpec(memory_space=pl.ANY),
                  pl.BlockSpec(memory_space=pl.ANY),
                  pl.BlockSpec((None, d, tf), lambda r, f, te, tv: (te[r], 0, fsel(f, tv, r))),
                  pl.BlockSpec((None, d, tf), lambda r, f, te, tv: (te[r], 0, fsel(f, tv, r))),
                  pl.BlockSpec((None, tf, d), lambda r, f, te, tv: (te[r], fsel(f, tv, r), 0))],
        out_specs=pl.BlockSpec(memory_space=pl.ANY),
        scratch_shapes=[pltpu.VMEM((tm * n_lane_tiles, LANES), F32), pltpu.VMEM((tm, d), BF16),
                        pltpu.VMEM((tm, d), F32), pltpu.VMEM((tm * n_lane_tiles, LANES), F32),
                        pltpu.SMEM((4 * tm,), jnp.int32), pltpu.SMEM((4 * tm,), jnp.int32),
                        pltpu.SemaphoreType.DMA((2,)), pltpu.SemaphoreType.DMA, pltpu.SemaphoreType.DMA],
    )
    return pl.pallas_call(
        functools.partial(_moe_ffn_kernel, n_tiles=n_tiles, nf=nf),
        grid_spec=grid_spec,
        out_shape=jax.ShapeDtypeStruct((TOP_K * t + tm, n_lane_tiles, LANES), F32),
        compiler_params=_cparams(("arbitrary", "arbitrary")),
        name="moe_ffn",
    )(tile_expert, tile_valid, tok, ydst, h2, wg, wu, wd)


def _combine_kernel(x_ref, y0_ref, y1_ref, route_ref, o_ref):
    route = route_ref[...]
    lane = lax.broadcasted_iota(jnp.int32, route.shape, 1)
    w0 = jnp.sum(jnp.where(lane == 0, route, 0.0), axis=1, keepdims=True)
    w1 = jnp.sum(jnp.where(lane == 1, route, 0.0), axis=1, keepdims=True)
    tm, d = x_ref.shape
    n_lane_tiles = d // LANES
    for j in range(n_lane_tiles):
        cols = slice(j * LANES, (j + 1) * LANES)
        rows = pl.ds(j, tm, stride=n_lane_tiles)
        o_ref[:, cols] = x_ref[:, cols] + w0 * y0_ref[rows, :] + w1 * y1_ref[rows, :]


def _combine(x1, y, route, tm=512):
    t, d = x1.shape
    nblk = t // tm
    return pl.pallas_call(
        _combine_kernel,
        grid=(nblk,),
        in_specs=[pl.BlockSpec((tm, d), lambda i: (i, 0)),
                  pl.BlockSpec((tm * (d // LANES), LANES), lambda i: (i, 0)),
                  pl.BlockSpec((tm * (d // LANES), LANES), lambda i: (i + nblk, 0)),
                  pl.BlockSpec((tm, LANES), lambda i: (i, 0))],
        out_specs=pl.BlockSpec((tm, d), lambda i: (i, 0)),
        out_shape=jax.ShapeDtypeStruct((t, d), F32),
        compiler_params=_cparams(("parallel",)),
        name="moe_combine",
    )(x1, y, y, route)


def _moe(h2, x1, route, wg, wu, wd, tm=1024):
    tok, ydst, tile_expert, tile_valid = _moe_plan(route, wg.shape[0], tm)
    y = _moe_ffn(h2, tok, ydst, tile_expert, tile_valid, wg, wu, wd)
    return _combine(x1, y.reshape(-1, LANES), route)


def _block_diag(w):
    nb, n, _ = w.shape
    eye = jnp.eye(nb, dtype=w.dtype)
    return (eye[:, None, :, None] * w[:, :, None, :]).reshape(nb * n, nb * n)


def kernel(x, norm_mix, w_in, conv_a_w, wo_a, q_norm, k_norm, wo_b, conv_c_w, conv_c_b, rg_w_r, rg_b_r, rg_w_i, rg_b_i, rg_lambda, wo_c, w_o, norm_ffn, ffn_w_gate, ffn_w_up, ffn_w_down, moe_router, moe_w_gate, moe_w_up, moe_w_down):
    bsz, s, d = x.shape
    depth = w_in.shape[0]
    cos_t, sa_t, sb_t = _rotary_tables(s)
    x2 = x.reshape(bsz * s, d)
    for l in range(depth):
        p2 = _inproj(x2, norm_mix[l][None, :], w_in[l].astype(BF16))
        p3 = p2.reshape(bsz, s, p2.shape[1])
        ya, yc = _seqmix(p3, conv_a_w[l], conv_c_w[l], conv_c_b[l][None, :],
                         _block_diag(rg_w_r[l]).astype(BF16), rg_b_r[l][None, :],
                         _block_diag(rg_w_i[l]).astype(BF16), rg_b_i[l][None, :],
                         rg_lambda[l][None, :])
        yb = _attention(p3, cos_t, sa_t, sb_t,
                        jnp.tile(q_norm[l], HEADS_PER_TILE)[None, :],
                        jnp.tile(k_norm[l], HEADS_PER_TILE)[None, :])
        cw = ya.shape[2]
        args = (x2, ya.reshape(bsz * s, cw), yb.reshape(bsz * s, yb.shape[2]), yc.reshape(bsz * s, cw), p2,
                wo_a[l].astype(BF16), wo_b[l].astype(BF16), wo_c[l].astype(BF16), w_o[l].astype(BF16),
                norm_ffn[l][None, :])
        if l % 2 == 0:
            x1, h2 = _merge(*args)
            x2 = _ffn(h2, x1, ffn_w_gate[l // 2].astype(BF16), ffn_w_up[l // 2].astype(BF16),
                      ffn_w_down[l // 2].astype(BF16))
        else:
            x1, h2, route = _merge(*args, router=moe_router[l // 2])
            x2 = _moe(h2, x1, route, moe_w_gate[l // 2].astype(BF16), moe_w_up[l // 2].astype(BF16),
                      moe_w_down[l // 2].astype(BF16))
    return x2.reshape(bsz, s, d)
```
